```python
import jax, jax.numpy as jnp
from jax import lax
import numpy as np

D_MODEL = 1024
BATCH = 8
SEQ = 4096
DEPTH = 4

D_MIX = D_MODEL
HG_HEADS = 4
HG_DK = 128
HG_DV = 128
HG_WIDTH = HG_HEADS * HG_DV
HG_CHUNK = 64
SA_HEADS = 4
SA_HEAD_DIM = 64
SA_WIDTH = SA_HEADS * SA_HEAD_DIM
IDX_HEADS = 8
IDX_DIM = 32
IDX_TOPK_MAX = 256
IDX_TOPK_DIV = 4
Q_BLOCK = 128
CV_WIDTH = D_MIX - HG_WIDTH - SA_WIDTH
CV_WIDTH_FILTER = 31
D_FF = 4 * D_MODEL
N_MOD = 6
EPS = 1e-6
NEG_BIG = -1e30
TINY = 1e-30

IN_SIZES = (
    HG_HEADS * HG_DK,
    HG_HEADS * HG_DK,
    HG_WIDTH,
    HG_WIDTH,
    SA_WIDTH,
    SA_WIDTH,
    SA_WIDTH,
    IDX_HEADS * IDX_DIM,
    IDX_DIM,
    IDX_HEADS,
    2 * CV_WIDTH,
)
D_IN = sum(IN_SIZES)

kernel_name = "hymba_hgrn2_dsa_conformer_trunk"


def rmsnorm(x, w):
    xf = x.astype(jnp.float32)
    y = xf * lax.rsqrt(jnp.mean(xf * xf, axis=-1, keepdims=True) + EPS)
    return (y * w.astype(jnp.float32)).astype(x.dtype)


def hgrn2_lower_bounds(lb_logits):
    p = jax.nn.softmax(lb_logits.astype(jnp.float32), axis=0)
    return jnp.cumsum(p, axis=0) - p[0]


def hgrn2_mixer(q_raw, f_raw, i_in, g_raw, lb, onorm_w):
    f32 = jnp.float32
    dt = q_raw.dtype
    bsz, L, _ = q_raw.shape
    lb = lb.astype(f32)
    fr = f_raw.astype(f32)
    q = jax.nn.silu(q_raw.astype(f32)) * (HG_DK ** -0.5)
    f = lb + (1.0 - lb) * jax.nn.sigmoid(fr)
    log_f = jnp.log(jnp.maximum(f, TINY))
    k = (1.0 - lb) * jax.nn.sigmoid(-fr)
    v = i_in.astype(f32)
    n_chunks = L // HG_CHUNK

    def to_chunks(t, d):
        return t.reshape(bsz, n_chunks, HG_CHUNK, HG_HEADS, d).transpose(1, 0, 3, 2, 4)

    causal = jnp.tril(jnp.ones((HG_CHUNK, HG_CHUNK), dtype=bool))

    def step(S, xs):
        qc, kc, gc, vc = xs
        G = jnp.cumsum(gc, axis=-2)
        o_inter = jnp.einsum('bhik,bhkv->bhiv', qc * jnp.exp(G), S)
        diff = G[:, :, :, None, :] - G[:, :, None, :, :]
        decay = jnp.exp(jnp.where(causal[:, :, None], diff, NEG_BIG))
        A = jnp.einsum('bhik,bhijk,bhjk->bhij', qc, decay, kc)
        o_intra = jnp.einsum('bhij,bhjv->bhiv', A, vc)
        G_last = G[:, :, -1:, :]
        S_new = jnp.exp(G_last[:, :, 0, :])[..., None] * S + jnp.einsum(
            'bhjk,bhjv->bhkv', kc * jnp.exp(G_last - G), vc)
        return S_new, o_inter + o_intra

    S0 = jnp.zeros((bsz, HG_HEADS, HG_DK, HG_DV), f32)
    _, o = lax.scan(step, S0, (to_chunks(q, HG_DK), to_chunks(k, HG_DK),
                               to_chunks(log_f, HG_DK), to_chunks(v, HG_DV)))
    o = o.transpose(1, 0, 3, 2, 4).reshape(bsz, L, HG_HEADS, HG_DV)
    o = o * lax.rsqrt(jnp.mean(o * o, axis=-1, keepdims=True) + EPS)
    o = o * onorm_w.astype(f32).reshape(HG_HEADS, HG_DV)
    o = o.reshape(bsz, L, HG_WIDTH) * jax.nn.silu(g_raw.astype(f32))
    return o.astype(dt)


def dsa_mixer(q, k, v, q_idx, k_idx, w_idx):
    f32 = jnp.float32
    dt = q.dtype
    bsz, L, _ = q.shape
    topk = min(IDX_TOPK_MAX, L // IDX_TOPK_DIV)
    n_blk = L // Q_BLOCK
    q = q.reshape(bsz, L, SA_HEADS, SA_HEAD_DIM)
    k = k.reshape(bsz, L, SA_HEADS, SA_HEAD_DIM)
    v = v.reshape(bsz, L, SA_HEADS, SA_HEAD_DIM)
    q_idx = q_idx.reshape(bsz, L, IDX_HEADS, IDX_DIM)
    w_idx = w_idx.astype(f32) * (IDX_HEADS ** -0.5)
    k_idx_f = k_idx.astype(f32)
    key_pos = jnp.arange(L, dtype=jnp.int32)

    def blocks(t):
        return t.reshape(bsz, n_blk, Q_BLOCK, *t.shape[2:]).swapaxes(0, 1)

    def one_block(xs):
        qb, qib, wb, t0 = xs
        qpos = t0 + jnp.arange(Q_BLOCK, dtype=jnp.int32)
        s = jnp.einsum('bqhd,bsd->bqhs', qib.astype(f32), k_idx_f) * (IDX_DIM ** -0.5)
        score = jnp.einsum('bqh,bqhs->bqs', wb, jax.nn.relu(s))
        admissible = key_pos[None, :] <= qpos[:, None]
        score = jnp.where(admissible[None], score, NEG_BIG)
        _, sel = lax.top_k(score, topk)
        valid = sel <= qpos[None, :, None]
        kg = jax.vmap(lambda kk, ii: kk[ii])(k, sel)
        vg = jax.vmap(lambda vv, ii: vv[ii])(v, sel)
        logits = jnp.einsum('bqhd,bqkhd->bqhk', qb.astype(f32), kg.astype(f32)) * (SA_HEAD_DIM ** -0.5)
        logits = jnp.where(valid[:, :, None, :], logits, NEG_BIG)
        p = jax.nn.softmax(logits, axis=-1)
        o = jnp.einsum('bqhk,bqkhd->bqhd', p, vg.astype(f32))
        return o.reshape(bsz, Q_BLOCK, SA_WIDTH).astype(dt)

    t0s = jnp.arange(n_blk, dtype=jnp.int32) * Q_BLOCK
    out = lax.map(one_block, (blocks(q), blocks(q_idx), blocks(w_idx), t0s))
    return out.swapaxes(0, 1).reshape(bsz, L, SA_WIDTH)


def conv_module(u, cv_w, cv_b, ln_w, ln_b):
    f32 = jnp.float32
    dt = u.dtype
    a, b = jnp.split(u, 2, axis=-1)
    h = a * jax.nn.sigmoid(b)
    h = lax.conv_general_dilated(
        h, cv_w[:, None, :], window_strides=(1,),
        padding=[(CV_WIDTH_FILTER - 1, 0)],
        dimension_numbers=('NWC', 'WIO', 'NWC'),
        feature_group_count=CV_WIDTH) + cv_b
    hf = h.astype(f32)
    mu = jnp.mean(hf, axis=-1, keepdims=True)
    var = jnp.mean(jnp.square(hf - mu), axis=-1, keepdims=True)
    hf = (hf - mu) * lax.rsqrt(var + EPS) * ln_w.astype(f32) + ln_b.astype(f32)
    return jax.nn.silu(hf).astype(dt)


def setup_inputs(seed: int = 0) -> dict:
    key = jax.random.key(seed)
    ks = jax.random.split(key, 17)
    f32 = jnp.float32
    n = lambda k, shape, s: jax.random.normal(k, shape, f32) * s
    return {
        "x": n(ks[0], (BATCH, SEQ, D_MODEL), 1.0),
        "c": n(ks[1], (BATCH, D_MODEL), 1.0),
        "ada_w": n(ks[2], (DEPTH, D_MODEL, N_MOD * D_MODEL), 0.5 * D_MODEL ** -0.5),
        "ada_b": n(ks[3], (DEPTH, N_MOD * D_MODEL), 0.02),
        "norm_mix_w": 1.0 + n(ks[4], (DEPTH, D_MODEL), 0.05),
        "norm_mlp_w": 1.0 + n(ks[5], (DEPTH, D_MODEL), 0.05),
        "w_in": n(ks[6], (DEPTH, D_MODEL, D_IN), D_MODEL ** -0.5),
        "hg_lb_logits": n(ks[7], (DEPTH, HG_HEADS * HG_DK), 1.0),
        "hg_onorm_w": 1.0 + n(ks[8], (DEPTH, HG_WIDTH), 0.05),
        "cv_w": n(ks[9], (DEPTH, CV_WIDTH_FILTER, CV_WIDTH), CV_WIDTH_FILTER ** -0.5),
        "cv_b": n(ks[10], (DEPTH, CV_WIDTH), 0.02),
        "cv_ln_w": 1.0 + n(ks[11], (DEPTH, CV_WIDTH), 0.05),
        "cv_ln_b": n(ks[12], (DEPTH, CV_WIDTH), 0.02),
        "w_out": n(ks[13], (DEPTH, D_MIX, D_MODEL), D_MIX ** -0.5),
        "mlp_w1": n(ks[14], (DEPTH, D_MODEL, D_FF), D_MODEL ** -0.5),
        "mlp_w2": n(ks[15], (DEPTH, D_FF, D_MODEL), D_FF ** -0.5),
        "final_norm_w": 1.0 + n(ks[16], (D_MODEL,), 0.05),
    }


def reference(x, c, ada_w, ada_b, norm_mix_w, norm_mlp_w, w_in, hg_lb_logits, hg_onorm_w,
              cv_w, cv_b, cv_ln_w, cv_ln_b, w_out, mlp_w1, mlp_w2, final_norm_w):
    offsets = np.cumsum(IN_SIZES)[:-1].tolist()
    lower_bounds = hgrn2_lower_bounds(hg_lb_logits)
    c_act = jax.nn.silu(c)
    for l in range(DEPTH):
        mod = c_act @ ada_w[l] + ada_b[l]
        sh1, sc1, g1, sh2, sc2, g2 = [m[:, None, :] for m in jnp.split(mod, N_MOD, axis=-1)]

        h = rmsnorm(x, norm_mix_w[l]) * (1.0 + sc1) + sh1
        proj = h @ w_in[l]
        (hq, hf, hi, hg, sq, sk, sv, iq, ik, iw, cu) = jnp.split(proj, offsets, axis=-1)
        o_a = hgrn2_mixer(hq, hf, hi, hg, lower_bounds[l], hg_onorm_w[l])
        o_b = dsa_mixer(sq, sk, sv, iq, ik, iw)
        o_c = conv_module(cu, cv_w[l], cv_b[l], cv_ln_w[l], cv_ln_b[l])
        mix = jnp.concatenate([o_a, o_b, o_c], axis=-1) @ w_out[l]
        x = x + g1 * mix

        h = rmsnorm(x, norm_mlp_w[l]) * (1.0 + sc2) + sh2
        ff = jnp.square(jax.nn.relu(h @ mlp_w1[l])) @ mlp_w2[l]
        x = x + g2 * ff
    return rmsnorm(x, final_norm_w)
```

```python
import functools

import jax
import jax.numpy as jnp
from jax import lax
from jax.experimental import pallas as pl
from jax.experimental.pallas import tpu as pltpu

f32 = jnp.float32
bf16 = jnp.bfloat16

D_MODEL = 1024
DEPTH = 4
HG_HEADS = 4
HG_DK = 128
HG_DV = 128
HG_WIDTH = HG_HEADS * HG_DV
HG_CHUNK = 64
HG_SUB = 16
SA_HEADS = 4
SA_HEAD_DIM = 64
SA_WIDTH = SA_HEADS * SA_HEAD_DIM
IDX_HEADS = 8
IDX_DIM = 32
IDX_TOPK_MAX = 256
IDX_TOPK_DIV = 4
CV_WIDTH = 256
CV_FILTER = 31
CV_HALO = 32
D_FF = 4 * D_MODEL
N_MOD = 6
EPS = 1e-6
NEG_BIG = -1e30
TINY = 1e-30

IN_SIZES = (512, 512, 512, 512, 256, 256, 256, 256, 32, 8, 512)
IKW_PAD = 128
D_IN_PAD = 2048 + 768 + 256 + 512 + IKW_PAD

VMEM_LIMIT = 56 * 1024 * 1024


def _dot(a, b):
    return jnp.dot(a, b, preferred_element_type=f32)


def _dot_nt(a, b):
    return lax.dot_general(a, b, (((1,), (1,)), ((), ())), preferred_element_type=f32)


def _silu(x):
    return x * jax.nn.sigmoid(x)


def _ada_kernel(c_ref, w_ref, b_ref, o_ref):
    ca = _silu(c_ref[...])
    o_ref[0] = _dot(ca.astype(bf16), w_ref[0].astype(bf16)) + b_ref[0]


def _ada(c, ada_w, ada_b):
    depth, d, _ = ada_w.shape
    bsz = c.shape[0]
    return pl.pallas_call(
        _ada_kernel,
        grid=(depth, N_MOD),
        in_specs=[
            pl.BlockSpec((bsz, d), lambda l, j: (0, 0)),
            pl.BlockSpec((1, d, d), lambda l, j: (l, 0, j)),
            pl.BlockSpec((1, 1, d), lambda l, j: (l, 0, j)),
        ],
        out_specs=pl.BlockSpec((1, bsz, d), lambda l, j: (l, 0, j)),
        out_shape=jax.ShapeDtypeStruct((depth, bsz, N_MOD * d), f32),
        name="ada",
    )(c, ada_w, ada_b.reshape(depth, 1, N_MOD * d))


def _modulated_norm(x, nw, sc, sh):
    ms = jnp.mean(x * x, axis=-1, keepdims=True)
    y = x * lax.rsqrt(ms + EPS) * nw
    return y * (1.0 + sc) + sh


def _premix_kernel(x_ref, sh_ref, sc_ref, nw_ref, w_ref,
                   hg_ref, sq_ref, sk_ref, sv_ref, iq_ref, cu_ref, ikw_ref):
    h = _modulated_norm(x_ref[0], nw_ref[...], sc_ref[0], sh_ref[0]).astype(bf16)
    hg_ref[0] = _dot(h, w_ref[:, 0:2048])
    sq_ref[0] = _dot(h, w_ref[:, 2048:2304]).astype(bf16)
    sk_ref[0] = _dot(h, w_ref[:, 2304:2560]).astype(bf16)
    sv_ref[0] = _dot(h, w_ref[:, 2560:2816]).astype(bf16)
    iq_ref[0] = _dot(h, w_ref[:, 2816:3072]).astype(bf16)
    cu_ref[0] = _dot(h, w_ref[:, 3072:3584])
    ikw_ref[0] = _dot(h, w_ref[:, 3584:3712])


def _premix(x, mod, nw, w_in_b, tm=512):
    bsz, L, d = x.shape
    tok = lambda w: pl.BlockSpec((1, tm, w), lambda b, i: (b, i, 0))
    return pl.pallas_call(
        _premix_kernel,
        grid=(bsz, L // tm),
        in_specs=[
            tok(d),
            pl.BlockSpec((1, 1, d), lambda b, i: (b, 0, 0)),
            pl.BlockSpec((1, 1, d), lambda b, i: (b, 0, 1)),
            pl.BlockSpec((1, d), lambda b, i: (0, 0)),
            pl.BlockSpec((d, D_IN_PAD), lambda b, i: (0, 0), pipeline_mode=pl.Buffered(1)),
        ],
        out_specs=[tok(2048), tok(256), tok(256), tok(256), tok(256), tok(512), tok(IKW_PAD)],
        out_shape=[
            jax.ShapeDtypeStruct((bsz, L, 2048), f32),
            jax.ShapeDtypeStruct((bsz, L, 256), bf16),
            jax.ShapeDtypeStruct((bsz, L, 256), bf16),
            jax.ShapeDtypeStruct((bsz, L, 256), bf16),
            jax.ShapeDtypeStruct((bsz, L, 256), bf16),
            jax.ShapeDtypeStruct((bsz, L, 512), f32),
            jax.ShapeDtypeStruct((bsz, L, IKW_PAD), f32),
        ],
        compiler_params=pltpu.CompilerParams(
            dimension_semantics=("parallel", "parallel"), vmem_limit_bytes=VMEM_LIMIT),
        name="premix",
    )(x, mod, mod, nw.reshape(1, d), w_in_b)


def _split3(x):
    hi = x.astype(bf16)
    r = x - hi.astype(f32)
    mid = r.astype(bf16)
    lo = (r - mid.astype(f32)).astype(bf16)
    return hi, mid, lo


def _hgrn2_chunk(qr, fr, v, gr, lb, onw, st):
    C = HG_CHUNK
    nsub = C // HG_SUB
    q = _silu(qr) * (HG_DK ** -0.5)
    f = lb + (1.0 - lb) * jax.nn.sigmoid(fr)
    logf = jnp.log(jnp.maximum(f, TINY))
    k = (1.0 - lb) * jax.nn.sigmoid(-fr)

    row = lax.broadcasted_iota(jnp.int32, (C, C), 0)
    col = lax.broadcasted_iota(jnp.int32, (C, C), 1)
    tril = jnp.where(col <= row, 1.0, 0.0).astype(bf16)
    hi, mid, lo = _split3(logf)
    G = _dot(tril, hi) + _dot(tril, mid) + _dot(tril, lo)
    g_last = G[C - 1:C, :]

    o = _dot_nt((q * jnp.exp(G)).astype(bf16), st.astype(bf16))

    lane = lax.broadcasted_iota(jnp.int32, (HG_SUB, C), 1)
    srow = lax.broadcasted_iota(jnp.int32, (HG_SUB, C), 0)
    ones = jnp.ones((HG_DK, C), bf16)
    a_rows = []
    for blk in range(nsub):
        r0 = blk * HG_SUB
        qb, kb, gb = q[r0:r0 + HG_SUB], k[r0:r0 + HG_SUB], G[r0:r0 + HG_SUB]
        parts = []
        for j in range(HG_SUB):
            d = jnp.minimum(gb - gb[j:j + 1, :], 0.0)
            parts.append(qb * kb[j:j + 1, :] * jnp.exp(d))
        rs = _dot(jnp.concatenate(parts, axis=0).astype(bf16), ones)
        a = jnp.zeros((HG_SUB, C), f32)
        for j in range(HG_SUB):
            a = jnp.where(lane == r0 + j, rs[j * HG_SUB:(j + 1) * HG_SUB, :], a)
        a = jnp.where(lane - r0 <= srow, a, 0.0)
        if blk > 0:
            gref = G[r0 - 1:r0, :]
            qd = qb * jnp.exp(gb - gref)
            kd = k * jnp.exp(jnp.minimum(gref - G, 0.0))
            off = _dot_nt(qd.astype(bf16), kd.astype(bf16))
            a = jnp.where(lane < r0, off, a)
        a_rows.append(a)
    A = jnp.concatenate(a_rows, axis=0)
    o = o + _dot(A.astype(bf16), v.astype(bf16))

    kg = k * jnp.exp(g_last - G)
    st_new = st * jnp.exp(g_last) + _dot(v.T.astype(bf16), kg.astype(bf16))

    o = o * lax.rsqrt(jnp.mean(o * o, axis=-1, keepdims=True) + EPS)
    o = o * onw * _silu(gr)
    return o, st_new


def _hgrn2_kernel(q_ref, f_ref, v_ref, g_ref, lb_ref, onw_ref, o_ref, st_ref, *, n_chunks):
    @pl.when(pl.program_id(2) == 0)
    def _():
        st_ref[...] = jnp.zeros_like(st_ref)

    lb = lb_ref[...]
    onw = onw_ref[...]

    def body(c, carry):
        r0 = pl.multiple_of(c * HG_CHUNK, HG_CHUNK)
        sl = pl.ds(r0, HG_CHUNK)
        o, st_new = _hgrn2_chunk(q_ref[0, sl, :], f_ref[0, sl, :], v_ref[0, sl, :], g_ref[0, sl, :],
                                 lb, onw, st_ref[...])
        st_ref[...] = st_new
        o_ref[0, sl, :] = o.astype(o_ref.dtype)
        return carry

    lax.fori_loop(0, n_chunks, body, 0)


def _hgrn2(hg, lb, onw, tl=512):
    bsz, L, _ = hg.shape
    part = lambda p: pl.BlockSpec((1, tl, HG_DK), lambda b, h, i: (b, i, p * HG_HEADS + h))
    vec = pl.BlockSpec((1, HG_DK), lambda b, h, i: (0, h))
    return pl.pallas_call(
        functools.partial(_hgrn2_kernel, n_chunks=tl // HG_CHUNK),
        grid=(bsz, HG_HEADS, L // tl),
        in_specs=[part(0), part(1), part(2), part(3), vec, vec],
        out_specs=pl.BlockSpec((1, tl, HG_DV), lambda b, h, i: (b, i, h)),
        out_shape=jax.ShapeDtypeStruct((bsz, L, HG_WIDTH), bf16),
        scratch_shapes=[pltpu.VMEM((HG_DV, HG_DK), f32)],
        compiler_params=pltpu.CompilerParams(
            dimension_semantics=("parallel", "parallel", "arbitrary"), vmem_limit_bytes=VMEM_LIMIT),
        name="hgrn2",
    )(hg, hg, hg, hg, lb.reshape(1, -1), onw.reshape(1, -1))


def _key_to_float(v):
    u = v ^ jnp.int32(-2 ** 31)
    bits = u ^ ((u >> 31) & jnp.int32(0x7FFFFFFF))
    return pltpu.bitcast(bits, f32)


def _dsa_kernel(qT_ref, k_ref, vT_ref, iqT_ref, ik_ref, wT_ref, o_ref, sc_ref, lg_ref, oT_ref,
                *, topk, qb, tk):
    i = pl.program_id(1)
    t0 = i * qb
    nk = (t0 + qb) // tk
    sub = 64
    qpos = t0 + lax.broadcasted_iota(jnp.int32, (1, qb), 1)

    def score_tile(kt, carry):
        for r in range(tk // sub):
            s0 = pl.multiple_of(kt * tk + r * sub, sub)
            ikt = ik_ref[0, pl.ds(s0, sub), :]
            acc = jnp.zeros((sub, qb), f32)
            for h in range(IDX_HEADS):
                sh = _dot(ikt, iqT_ref[0, h * IDX_DIM:(h + 1) * IDX_DIM, :])
                acc = acc + wT_ref[0, h:h + 1, :] * jnp.maximum(sh, 0.0)
            kpos = s0 + lax.broadcasted_iota(jnp.int32, (sub, qb), 0)
            sc_ref[pl.ds(s0, sub), :] = jnp.where(kpos <= qpos, acc + 0.0, NEG_BIG)
        return carry

    lax.fori_loop(0, nk, score_tile, 0)

    def count(indicator):
        def tile(kt, acc):
            s0 = pl.multiple_of(kt * tk, tk)
            ones = indicator(sc_ref[pl.ds(s0, tk), :], s0)
            return acc + jnp.sum(ones.reshape(tk // 8, 8, qb), axis=0)
        acc = lax.fori_loop(0, nk, tile, jnp.zeros((8, qb), f32))
        return jnp.sum(acc, axis=0, keepdims=True)

    def bis(it, v):
        trial = v | (jnp.int32(1) << (31 - it))
        cand = _key_to_float(trial)
        cnt = count(lambda s, s0: jnp.where(s >= cand, 1.0, 0.0))
        return jnp.where(cnt >= topk, trial, v)

    vkey = lax.fori_loop(0, 32, bis, jnp.zeros((1, qb), jnp.int32))
    thr = _key_to_float(vkey)
    need = topk - count(lambda s, s0: jnp.where(s > thr, 1.0, 0.0))
    cnt_ge = count(lambda s, s0: jnp.where(s >= thr, 1.0, 0.0))

    def tie_limit(_):
        def step(it, p):
            trial = p | (jnp.int32(1) << (12 - it))
            def tied_before(s, s0):
                kpos = s0 + lax.broadcasted_iota(jnp.int32, (tk, qb), 0)
                return jnp.where(kpos < trial, jnp.where(s == thr, 1.0, 0.0), 0.0)
            return jnp.where(count(tied_before) < need, trial, p)
        return lax.fori_loop(0, 13, step, jnp.zeros((1, qb), jnp.int32))

    has_excess = jnp.max(cnt_ge) > topk
    plim = lax.cond(has_excess, tie_limit, lambda _: jnp.full((1, qb), 2 ** 30, jnp.int32), 0)
    take_all = qpos < topk

    def bias_tile(kt, carry):
        s0 = pl.multiple_of(kt * tk, tk)
        s = sc_ref[pl.ds(s0, tk), :]
        kpos = s0 + lax.broadcasted_iota(jnp.int32, (tk, qb), 0)
        tie = jnp.where(kpos <= plim, 0.0, NEG_BIG)
        bias = jnp.where(s > thr, 0.0, jnp.where(s == thr, tie, NEG_BIG))
        bias = jnp.where(take_all, 0.0, bias)
        sc_ref[pl.ds(s0, tk), :] = jnp.where(kpos <= qpos, bias, NEG_BIG)
        return carry

    lax.fori_loop(0, nk, bias_tile, 0)

    q_all = qT_ref[0] * jnp.asarray(SA_HEAD_DIM ** -0.5, bf16)
    hrow = lax.broadcasted_iota(jnp.int32, q_all.shape, 0) // SA_HEAD_DIM
    for h in range(SA_HEADS):
        q_h = jnp.where(hrow == h, q_all, jnp.zeros_like(q_all))

        def logit_tile(kt, mx):
            s0 = pl.multiple_of(kt * tk, tk)
            lg = _dot(k_ref[0, pl.ds(s0, tk), :], q_h) + sc_ref[pl.ds(s0, tk), :]
            lg_ref[pl.ds(s0, tk), :] = lg
            return jnp.maximum(mx, jnp.max(lg.reshape(tk // 8, 8, qb), axis=0))

        mx = lax.fori_loop(0, nk, logit_tile, jnp.full((8, qb), NEG_BIG, f32))
        mx = jnp.max(mx, axis=0, keepdims=True)

        def pv_tile(kt, carry):
            acc, den = carry
            s0 = pl.multiple_of(kt * tk, tk)
            p = jnp.exp(lg_ref[pl.ds(s0, tk), :] - mx)
            den = den + jnp.sum(p.reshape(tk // 8, 8, qb), axis=0)
            vt = vT_ref[0, h * SA_HEAD_DIM:(h + 1) * SA_HEAD_DIM, pl.ds(s0, tk)]
            return acc + _dot(vt, p.astype(bf16)), den

        acc, den = lax.fori_loop(0, nk, pv_tile,
                                 (jnp.zeros((SA_HEAD_DIM, qb), f32), jnp.zeros((8, qb), f32)))
        oT_ref[h * SA_HEAD_DIM:(h + 1) * SA_HEAD_DIM, :] = acc / jnp.sum(den, axis=0, keepdims=True)

    o_ref[0] = oT_ref[...].T.astype(o_ref.dtype)


def _dsa(qT, k, vT, iqT, ik, wT, qb=256):
    bsz, L, _ = k.shape
    topk = min(IDX_TOPK_MAX, L // IDX_TOPK_DIV)
    return pl.pallas_call(
        functools.partial(_dsa_kernel, topk=topk, qb=qb, tk=qb),
        grid=(bsz, L // qb),
        in_specs=[
            pl.BlockSpec((1, SA_WIDTH, qb), lambda b, i: (b, 0, i)),
            pl.BlockSpec((1, L, SA_WIDTH), lambda b, i: (b, 0, 0)),
            pl.BlockSpec((1, SA_WIDTH, L), lambda b, i: (b, 0, 0)),
            pl.BlockSpec((1, IDX_HEADS * IDX_DIM, qb), lambda b, i: (b, 0, i)),
            pl.BlockSpec((1, L, IDX_DIM), lambda b, i: (b, 0, 0)),
            pl.BlockSpec((1, IDX_HEADS, qb), lambda b, i: (b, 0, i)),
        ],
        out_specs=pl.BlockSpec((1, qb, SA_WIDTH), lambda b, i: (b, i, 0)),
        out_shape=jax.ShapeDtypeStruct((bsz, L, SA_WIDTH), bf16),
        scratch_shapes=[pltpu.VMEM((L, qb), f32), pltpu.VMEM((L, qb), f32),
                        pltpu.VMEM((SA_WIDTH, qb), f32)],
        compiler_params=pltpu.CompilerParams(
            dimension_semantics=("parallel", "arbitrary"), vmem_limit_bytes=VMEM_LIMIT),
        name="dsa",
    )(qT, k, vT, iqT, ik, wT)


def _conv_kernel(cu_ref, halo_ref, w_ref, b_ref, lnw_ref, lnb_ref, o_ref, h_ref, *, tl, strip):
    glu = lambda u: u[:, :CV_WIDTH] * jax.nn.sigmoid(u[:, CV_WIDTH:])
    halo = glu(halo_ref[0])
    h_ref[0:CV_HALO, :] = jnp.where(pl.program_id(1) > 0, halo, 0.0)
    h_ref[CV_HALO:, :] = glu(cu_ref[0])
    lead = CV_HALO - (CV_FILTER - 1)
    for r in range(tl // strip):
        acc = jnp.zeros((strip, CV_WIDTH), f32) + b_ref[...]
        for j in range(CV_FILTER):
            acc = acc + w_ref[j:j + 1, :] * h_ref[r * strip + lead + j:r * strip + lead + j + strip, :]
        mu = jnp.mean(acc, axis=-1, keepdims=True)
        cen = acc - mu
        var = jnp.mean(cen * cen, axis=-1, keepdims=True)
        y = cen * lax.rsqrt(var + EPS) * lnw_ref[...] + lnb_ref[...]
        o_ref[0, r * strip:(r + 1) * strip, :] = _silu(y).astype(o_ref.dtype)


def _conv(cu, cv_w, cv_b, ln_w, ln_b, tl=512, strip=64):
    bsz, L, _ = cu.shape
    hb = tl // CV_HALO
    vec = pl.BlockSpec((1, CV_WIDTH), lambda b, i: (0, 0))
    return pl.pallas_call(
        functools.partial(_conv_kernel, tl=tl, strip=strip),
        grid=(bsz, L // tl),
        in_specs=[
            pl.BlockSpec((1, tl, 2 * CV_WIDTH), lambda b, i: (b, i, 0)),
            pl.BlockSpec((1, CV_HALO, 2 * CV_WIDTH), lambda b, i: (b, jnp.maximum(i * hb - 1, 0), 0)),
            pl.BlockSpec((CV_FILTER, CV_WIDTH), lambda b, i: (0, 0)),
            vec, vec, vec,
        ],
        out_specs=pl.BlockSpec((1, tl, CV_WIDTH), lambda b, i: (b, i, 0)),
        out_shape=jax.ShapeDtypeStruct((bsz, L, CV_WIDTH), bf16),
        scratch_shapes=[pltpu.VMEM((tl + CV_HALO, CV_WIDTH), f32)],
        compiler_params=pltpu.CompilerParams(dimension_semantics=("parallel", "parallel")),
        name="conv",
    )(cu, cu, cv_w, cv_b.reshape(1, -1), ln_w.reshape(1, -1), ln_b.reshape(1, -1))


def _postmix_kernel(x_ref, oa_ref, ob_ref, oc_ref, g1_ref, sh_ref, sc_ref, g2_ref, nw_ref,
                    wo_ref, w1_ref, w2_ref, fw_ref, o_ref, acc_ref, *, ff_chunk, final):
    mix = (_dot(oa_ref[0], wo_ref[0:HG_WIDTH, :])
           + _dot(ob_ref[0], wo_ref[HG_WIDTH:HG_WIDTH + SA_WIDTH, :])
           + _dot(oc_ref[0], wo_ref[HG_WIDTH + SA_WIDTH:, :]))
    x1 = x_ref[0] + g1_ref[0] * mix
    h = _modulated_norm(x1, nw_ref[...], sc_ref[0], sh_ref[0]).astype(bf16)
    for c in range(D_FF // ff_chunk):
        u = jnp.maximum(_dot(h, w1_ref[:, c * ff_chunk:(c + 1) * ff_chunk]), 0.0)
        y = _dot((u * u).astype(bf16), w2_ref[c * ff_chunk:(c + 1) * ff_chunk, :])
        if c == 0:
            acc_ref[...] = y
        else:
            acc_ref[...] += y
    x2 = x1 + g2_ref[0] * acc_ref[...]
    if final:
        ms = jnp.mean(x2 * x2, axis=-1, keepdims=True)
        x2 = x2 * lax.rsqrt(ms + EPS) * fw_ref[...]
    o_ref[0] = x2


def _postmix(x, o_a, o_b, o_c, mod, nw, wo_b, w1_b, w2_b, fw, final, tm=512, ff_chunk=1024):
    bsz, L, d = x.shape
    tok = lambda w: pl.BlockSpec((1, tm, w), lambda b, i: (b, i, 0))
    modp = lambda j: pl.BlockSpec((1, 1, d), lambda b, i: (b, 0, j))
    const = lambda shape: pl.BlockSpec(shape, lambda b, i: (0, 0), pipeline_mode=pl.Buffered(1))
    return pl.pallas_call(
        functools.partial(_postmix_kernel, ff_chunk=ff_chunk, final=final),
        grid=(bsz, L // tm),
        in_specs=[
            tok(d), tok(HG_WIDTH), tok(SA_WIDTH), tok(CV_WIDTH),
            modp(2), modp(3), modp(4), modp(5),
            pl.BlockSpec((1, d), lambda b, i: (0, 0)),
            const((d, d)), const((d, D_FF)), const((D_FF, d)),
            pl.BlockSpec((1, d), lambda b, i: (0, 0)),
        ],
        out_specs=tok(d),
        out_shape=jax.ShapeDtypeStruct((bsz, L, d), f32),
        scratch_shapes=[pltpu.VMEM((tm, d), f32)],
        compiler_params=pltpu.CompilerParams(
            dimension_semantics=("parallel", "parallel"), vmem_limit_bytes=VMEM_LIMIT),
        name="postmix",
    )(x, o_a, o_b, o_c, mod, mod, mod, mod, nw.reshape(1, d), wo_b, w1_b, w2_b, fw.reshape(1, d))


def _pack_w_in(w_in):
    pad = jnp.zeros(w_in.shape[:-1] + (D_IN_PAD - w_in.shape[-1],), w_in.dtype)
    return jnp.concatenate([w_in[..., :3072], w_in[..., 3112:], w_in[..., 3072:3112], pad], axis=-1).astype(bf16)


def _lower_bounds(lb_logits):
    p = jax.nn.softmax(lb_logits.astype(f32), axis=0)
    return jnp.cumsum(p, axis=0) - p[0]


def kernel(x, c, ada_w, ada_b, norm_mix_w, norm_mlp_w, w_in, hg_lb_logits, hg_onorm_w, cv_w, cv_b,
           cv_ln_w, cv_ln_b, w_out, mlp_w1, mlp_w2, final_norm_w):
    bsz, L, d = x.shape
    depth = w_in.shape[0]
    mods = _ada(c, ada_w, ada_b)
    lbs = _lower_bounds(hg_lb_logits)
    w_in_b = _pack_w_in(w_in)
    wo_b, w1_b, w2_b = w_out.astype(bf16), mlp_w1.astype(bf16), mlp_w2.astype(bf16)
    tr = lambda a: jnp.swapaxes(a, 1, 2)
    for l in range(depth):
        mod = mods[l].reshape(bsz, 1, N_MOD * d)
        hg, sq, sk, sv, iq, cu, ikw = _premix(x, mod, norm_mix_w[l], w_in_b[l])
        o_a = _hgrn2(hg, lbs[l], hg_onorm_w[l])
        o_b = _dsa(tr(sq), sk, tr(sv), tr(iq), ikw[..., :IDX_DIM].astype(bf16),
                   tr(ikw[..., IDX_DIM:IDX_DIM + IDX_HEADS]))
        o_c = _conv(cu, cv_w[l], cv_b[l], cv_ln_w[l], cv_ln_b[l])
        x = _postmix(x, o_a, o_b, o_c, mod, norm_mlp_w[l], wo_b[l], w1_b[l], w2_b[l],
                     final_norm_w, final=(l == depth - 1))
    return x
```

```python
import functools

import jax
import jax.numpy as jnp
from jax import lax
from jax.experimental import pallas as pl
from jax.experimental.pallas import tpu as pltpu

f32 = jnp.float32
bf16 = jnp.bfloat16

D_MODEL = 1024
DEPTH = 4
HG_HEADS = 4
HG_DK = 128
HG_DV = 128
HG_WIDTH = HG_HEADS * HG_DV
HG_CHUNK = 64
HG_SUB = 8
SA_HEADS = 4
SA_HEAD_DIM = 64
SA_WIDTH = SA_HEADS * SA_HEAD_DIM
IDX_HEADS = 8
IDX_DIM = 32
IDX_TOPK_MAX = 256
IDX_TOPK_DIV = 4
CV_WIDTH = 256
CV_FILTER = 31
CV_HALO = 32
CNT_ROWS = 64
D_FF = 4 * D_MODEL
N_MOD = 6
EPS = 1e-6
NEG_BIG = -1e30
TINY = 1e-30

IN_SIZES = (512, 512, 512, 512, 256, 256, 256, 256, 32, 8, 512)
IKW_PAD = 128
D_IN_PAD = 2048 + 768 + 256 + 512 + IKW_PAD

VMEM_LIMIT = 56 * 1024 * 1024


def _dot(a, b):
    return jnp.dot(a, b, preferred_element_type=f32)


def _dot_nt(a, b):
    return lax.dot_general(a, b, (((1,), (1,)), ((), ())), preferred_element_type=f32)


def _silu(x):
    return x * jax.nn.sigmoid(x)


def _ada_kernel(c_ref, w_ref, b_ref, o_ref):
    ca = _silu(c_ref[...])
    o_ref[0] = _dot(ca.astype(bf16), w_ref[0].astype(bf16)) + b_ref[0]


def _ada(c, ada_w, ada_b):
    depth, d, _ = ada_w.shape
    bsz = c.shape[0]
    return pl.pallas_call(
        _ada_kernel,
        grid=(depth, N_MOD),
        in_specs=[
            pl.BlockSpec((bsz, d), lambda l, j: (0, 0)),
            pl.BlockSpec((1, d, d), lambda l, j: (l, 0, j)),
            pl.BlockSpec((1, 1, d), lambda l, j: (l, 0, j)),
        ],
        out_specs=pl.BlockSpec((1, bsz, d), lambda l, j: (l, 0, j)),
        out_shape=jax.ShapeDtypeStruct((depth, bsz, N_MOD * d), f32),
        name="ada",
    )(c, ada_w, ada_b.reshape(depth, 1, N_MOD * d))


def _modulated_norm(x, nw, sc, sh):
    ms = jnp.mean(x * x, axis=-1, keepdims=True)
    y = x * lax.rsqrt(ms + EPS) * nw
    return y * (1.0 + sc) + sh


def _premix_kernel(x_ref, sh_ref, sc_ref, nw_ref, w_ref,
                   hg_ref, sq_ref, sk_ref, sv_ref, iq_ref, cu_ref, ikw_ref):
    h = _modulated_norm(x_ref[0], nw_ref[...], sc_ref[0], sh_ref[0]).astype(bf16)
    hg_ref[0] = _dot(h, w_ref[:, 0:2048])
    sq_ref[0] = _dot(h, w_ref[:, 2048:2304]).astype(bf16)
    sk_ref[0] = _dot(h, w_ref[:, 2304:2560]).astype(bf16)
    sv_ref[0] = _dot(h, w_ref[:, 2560:2816]).astype(bf16)
    iq_ref[0] = _dot(h, w_ref[:, 2816:3072]).astype(bf16)
    cu_ref[0] = _dot(h, w_ref[:, 3072:3584])
    ikw_ref[0] = _dot(h, w_ref[:, 3584:3712])


def _premix(x, mod, nw, w_in_b, tm=512):
    bsz, L, d = x.shape
    tok = lambda w: pl.BlockSpec((1, tm, w), lambda b, i: (b, i, 0))
    return pl.pallas_call(
        _premix_kernel,
        grid=(bsz, L // tm),
        in_specs=[
            tok(d),
            pl.BlockSpec((1, 1, d), lambda b, i: (b, 0, 0)),
            pl.BlockSpec((1, 1, d), lambda b, i: (b, 0, 1)),
            pl.BlockSpec((1, d), lambda b, i: (0, 0)),
            pl.BlockSpec((d, D_IN_PAD), lambda b, i: (0, 0), pipeline_mode=pl.Buffered(1)),
        ],
        out_specs=[tok(2048), tok(256), tok(256), tok(256), tok(256), tok(512), tok(IKW_PAD)],
        out_shape=[
            jax.ShapeDtypeStruct((bsz, L, 2048), f32),
            jax.ShapeDtypeStruct((bsz, L, 256), bf16),
            jax.ShapeDtypeStruct((bsz, L, 256), bf16),
            jax.ShapeDtypeStruct((bsz, L, 256), bf16),
            jax.ShapeDtypeStruct((bsz, L, 256), bf16),
            jax.ShapeDtypeStruct((bsz, L, 512), f32),
            jax.ShapeDtypeStruct((bsz, L, IKW_PAD), f32),
        ],
        compiler_params=pltpu.CompilerParams(
            dimension_semantics=("parallel", "parallel"), vmem_limit_bytes=VMEM_LIMIT),
        name="premix",
    )(x, mod, mod, nw.reshape(1, d), w_in_b)


def _split3(x):
    hi = x.astype(bf16)
    r = x - hi.astype(f32)
    mid = r.astype(bf16)
    lo = (r - mid.astype(f32)).astype(bf16)
    return hi, mid, lo


def _hgrn2_chunk(qr, fr, v, gr, lb, onw, st_ref):
    C, S, W = HG_CHUNK, HG_SUB, HG_DK
    nsub = C // S
    heads = range(HG_HEADS)
    hs = lambda x, h: x[:, h * W:(h + 1) * W]

    q = _silu(qr) * (HG_DK ** -0.5)
    f = lb + (1.0 - lb) * jax.nn.sigmoid(fr)
    lf2 = jnp.log2(jnp.maximum(f, TINY))
    k = (1.0 - lb) * jax.nn.sigmoid(-fr)
    lk2 = jnp.log2(k)

    row = lax.broadcasted_iota(jnp.int32, (C, C), 0)
    col = lax.broadcasted_iota(jnp.int32, (C, C), 1)
    tril = jnp.where(col <= row, 1.0, 0.0).astype(bf16)
    g3 = _dot(tril, jnp.concatenate(_split3(lf2), axis=1))
    nw = HG_HEADS * W
    G2 = g3[:, :nw] + g3[:, nw:2 * nw] + g3[:, 2 * nw:]
    H = lk2 - G2
    g2_last = G2[C - 1:C, :]

    lane = lax.broadcasted_iota(jnp.int32, (S, C), 1)
    srow = lax.broadcasted_iota(jnp.int32, (S, C), 0)
    ones = jnp.ones((W, C), bf16)
    zpad = lambda x: jnp.concatenate([x, jnp.zeros((C - x.shape[0], W), x.dtype)], axis=0)

    o_inter, rsums, offs = [], [], []
    for h in heads:
        qh, gh, hh, lkh = hs(q, h), hs(G2, h), hs(H, h), hs(lk2, h)
        o_inter.append(_dot_nt((qh * jnp.exp2(gh)).astype(bf16), st_ref[h].astype(bf16)))
        parts = []
        for blk in range(nsub):
            r0 = blk * S
            qb, gb = qh[r0:r0 + S], gh[r0:r0 + S]
            for j in range(r0, r0 + S):
                parts.append(qb * jnp.exp2(jnp.minimum(gb + hh[j:j + 1, :], lkh[j:j + 1, :])))
        rsums.append(_dot(jnp.concatenate(parts, axis=0).astype(bf16), ones))
        off_h = []
        for blk in range(1, nsub):
            r0 = blk * S
            gref = gh[r0 - 1:r0, :]
            qd = qh[r0:r0 + S] * jnp.exp2(gh[r0:r0 + S] - gref)
            kd = jnp.exp2(jnp.minimum(gref + hh[:r0], lkh[:r0]))
            off_h.append(_dot_nt(qd.astype(bf16), zpad(kd).astype(bf16)))
        offs.append(off_h)

    outs = []
    for h in heads:
        vh = hs(v, h)
        a_rows = []
        for blk in range(nsub):
            r0 = blk * S
            a = jnp.zeros((S, C), f32)
            for j in range(S):
                a = jnp.where(lane == r0 + j, rsums[h][(r0 + j) * S:(r0 + j + 1) * S, :], a)
            a = jnp.where(lane - r0 <= srow, a, 0.0)
            if blk > 0:
                a = a + offs[h][blk - 1]
            a_rows.append(a)
        A = jnp.concatenate(a_rows, axis=0)
        o = o_inter[h] + _dot(A.astype(bf16), vh.astype(bf16))
        kg = jnp.exp2(hs(g2_last, h) + hs(H, h))
        st_ref[h] = st_ref[h] * jnp.exp2(hs(g2_last, h)) + _dot(vh.T.astype(bf16), kg.astype(bf16))
        o = o * lax.rsqrt(jnp.mean(o * o, axis=-1, keepdims=True) + EPS)
        outs.append(o * hs(onw, h) * _silu(hs(gr, h)))
    return jnp.concatenate(outs, axis=1)


def _hgrn2_kernel(q_ref, f_ref, v_ref, g_ref, lb_ref, onw_ref, o_ref, st_ref, *, n_chunks):
    @pl.when(pl.program_id(1) == 0)
    def _():
        st_ref[...] = jnp.zeros_like(st_ref)

    lb = lb_ref[...]
    onw = onw_ref[...]

    def body(c, carry):
        r0 = pl.multiple_of(c * HG_CHUNK, HG_CHUNK)
        sl = pl.ds(r0, HG_CHUNK)
        o = _hgrn2_chunk(q_ref[0, sl, :], f_ref[0, sl, :], v_ref[0, sl, :], g_ref[0, sl, :], lb, onw, st_ref)
        o_ref[0, sl, :] = o.astype(o_ref.dtype)
        return carry

    lax.fori_loop(0, n_chunks, body, 0)


def _hgrn2(hg, lb, onw, tl=512):
    bsz, L, _ = hg.shape
    part = lambda p: pl.BlockSpec((1, tl, HG_WIDTH), lambda b, i: (b, i, p))
    vec = pl.BlockSpec((1, HG_WIDTH), lambda b, i: (0, 0))
    return pl.pallas_call(
        functools.partial(_hgrn2_kernel, n_chunks=tl // HG_CHUNK),
        grid=(bsz, L // tl),
        in_specs=[part(0), part(1), part(2), part(3), vec, vec],
        out_specs=pl.BlockSpec((1, tl, HG_WIDTH), lambda b, i: (b, i, 0)),
        out_shape=jax.ShapeDtypeStruct((bsz, L, HG_WIDTH), bf16),
        scratch_shapes=[pltpu.VMEM((HG_HEADS, HG_DV, HG_DK), f32)],
        compiler_params=pltpu.CompilerParams(
            dimension_semantics=("parallel", "arbitrary"), vmem_limit_bytes=VMEM_LIMIT),
        name="hgrn2",
    )(hg, hg, hg, hg, lb.reshape(1, -1), onw.reshape(1, -1))


def _key_to_float(v):
    u = v ^ jnp.int32(-2 ** 31)
    bits = u ^ ((u >> 31) & jnp.int32(0x7FFFFFFF))
    return pltpu.bitcast(bits, f32)


def _dsa_kernel(qT_ref, k_ref, vT_ref, iqT_ref, ik_ref, wT_ref, o_ref, sc_ref, qh_ref, mx_ref, den_ref,
                oT_ref, *, topk, qb, tk):
    i = pl.program_id(1)
    t0 = i * qb
    nk = (t0 + qb) // tk
    sub = 64
    qpos = t0 + lax.broadcasted_iota(jnp.int32, (1, qb), 1)

    def score_tile(kt, carry):
        for r in range(tk // sub):
            s0 = pl.multiple_of(kt * tk + r * sub, sub)
            ikt = ik_ref[0, pl.ds(s0, sub), :]
            acc = jnp.zeros((sub, qb), f32)
            for h in range(IDX_HEADS):
                sh = _dot(ikt, iqT_ref[0, h * IDX_DIM:(h + 1) * IDX_DIM, :])
                acc = acc + wT_ref[0, h:h + 1, :] * jnp.maximum(sh, 0.0)
            kpos = s0 + lax.broadcasted_iota(jnp.int32, (sub, qb), 0)
            sc_ref[pl.ds(s0, sub), :] = jnp.where(kpos <= qpos, acc + 0.0, NEG_BIG)
        return carry

    lax.fori_loop(0, nk, score_tile, 0)

    def count(indicator):
        def tile(kt, acc):
            s0 = pl.multiple_of(kt * tk, tk)
            ones = indicator(sc_ref[pl.ds(s0, tk), :], s0)
            return acc + jnp.sum(ones.reshape(tk // CNT_ROWS, CNT_ROWS, qb), axis=0)
        acc = lax.fori_loop(0, nk, tile, jnp.zeros((CNT_ROWS, qb), f32))
        return jnp.sum(acc, axis=0, keepdims=True)

    def bis(it, v):
        trial = v | (jnp.int32(1) << (31 - it))
        cand = _key_to_float(trial)
        cnt = count(lambda s, s0: jnp.where(s >= cand, 1.0, 0.0))
        return jnp.where(cnt >= topk, trial, v)

    vkey = lax.fori_loop(0, 32, bis, jnp.zeros((1, qb), jnp.int32))
    thr = _key_to_float(vkey)
    need = topk - count(lambda s, s0: jnp.where(s > thr, 1.0, 0.0))
    cnt_ge = count(lambda s, s0: jnp.where(s >= thr, 1.0, 0.0))

    def tie_limit(_):
        def step(it, p):
            trial = p | (jnp.int32(1) << (12 - it))
            def tied_before(s, s0):
                kpos = s0 + lax.broadcasted_iota(jnp.int32, (tk, qb), 0)
                return jnp.where(kpos < trial, jnp.where(s == thr, 1.0, 0.0), 0.0)
            return jnp.where(count(tied_before) < need, trial, p)
        return lax.fori_loop(0, 13, step, jnp.zeros((1, qb), jnp.int32))

    has_excess = jnp.max(cnt_ge) > topk
    plim = lax.cond(has_excess, tie_limit, lambda _: jnp.full((1, qb), 2 ** 30, jnp.int32), 0)
    take_all = qpos < topk

    def bias_tile(kt, carry):
        s0 = pl.multiple_of(kt * tk, tk)
        s = sc_ref[pl.ds(s0, tk), :]
        kpos = s0 + lax.broadcasted_iota(jnp.int32, (tk, qb), 0)
        tie = jnp.where(kpos <= plim, 0.0, NEG_BIG)
        bias = jnp.where(s > thr, 0.0, jnp.where(s == thr, tie, NEG_BIG))
        bias = jnp.where(take_all, 0.0, bias)
        sc_ref[pl.ds(s0, tk), :] = jnp.where(kpos <= qpos, bias, NEG_BIG)
        return carry

    lax.fori_loop(0, nk, bias_tile, 0)

    q_all = qT_ref[0] * jnp.asarray(SA_HEAD_DIM ** -0.5, bf16)
    hrow = lax.broadcasted_iota(jnp.int32, q_all.shape, 0) // SA_HEAD_DIM
    for h in range(SA_HEADS):
        qh_ref[:, h * qb:(h + 1) * qb] = jnp.where(hrow == h, q_all, jnp.zeros_like(q_all))
    mx_ref[...] = jnp.full(mx_ref.shape, NEG_BIG, f32)
    den_ref[...] = jnp.zeros_like(den_ref)
    oT_ref[...] = jnp.zeros_like(oT_ref)

    def att_tile(kt, carry):
        s0 = pl.multiple_of(kt * tk, tk)
        bias = sc_ref[pl.ds(s0, tk), :]
        lg_all = _dot(k_ref[0, pl.ds(s0, tk), :], qh_ref[...])
        ps, alphas = [], []
        for h in range(SA_HEADS):
            lg = lg_all[:, h * qb:(h + 1) * qb] + bias
            m_old = mx_ref[h:h + 1, :]
            m_new = jnp.maximum(m_old, jnp.max(lg, axis=0, keepdims=True))
            alpha = jnp.exp(m_old - m_new)
            p = jnp.exp(lg - m_new)
            mx_ref[h:h + 1, :] = m_new
            den_ref[h:h + 1, :] = alpha * den_ref[h:h + 1, :] + jnp.sum(p, axis=0, keepdims=True)
            ps.append(p.astype(bf16))
            alphas.append(alpha)
        for h in range(SA_HEADS):
            hs = slice(h * SA_HEAD_DIM, (h + 1) * SA_HEAD_DIM)
            oT_ref[hs, :] = alphas[h] * oT_ref[hs, :] + _dot(vT_ref[0, hs, pl.ds(s0, tk)], ps[h])
        return carry

    lax.fori_loop(0, nk, att_tile, 0)
    for h in range(SA_HEADS):
        hs = slice(h * SA_HEAD_DIM, (h + 1) * SA_HEAD_DIM)
        oT_ref[hs, :] = oT_ref[hs, :] / den_ref[h:h + 1, :]
    o_ref[0] = oT_ref[...].T.astype(o_ref.dtype)


def _dsa(qT, k, vT, iqT, ik, wT, qb=256):
    bsz, L, _ = k.shape
    topk = min(IDX_TOPK_MAX, L // IDX_TOPK_DIV)
    return pl.pallas_call(
        functools.partial(_dsa_kernel, topk=topk, qb=qb, tk=qb),
        grid=(bsz, L // qb),
        in_specs=[
            pl.BlockSpec((1, SA_WIDTH, qb), lambda b, i: (b, 0, i)),
            pl.BlockSpec((1, L, SA_WIDTH), lambda b, i: (b, 0, 0)),
            pl.BlockSpec((1, SA_WIDTH, L), lambda b, i: (b, 0, 0)),
            pl.BlockSpec((1, IDX_HEADS * IDX_DIM, qb), lambda b, i: (b, 0, i)),
            pl.BlockSpec((1, L, IDX_DIM), lambda b, i: (b, 0, 0)),
            pl.BlockSpec((1, IDX_HEADS, qb), lambda b, i: (b, 0, i)),
        ],
        out_specs=pl.BlockSpec((1, qb, SA_WIDTH), lambda b, i: (b, i, 0)),
        out_shape=jax.ShapeDtypeStruct((bsz, L, SA_WIDTH), bf16),
        scratch_shapes=[pltpu.VMEM((L, qb), f32), pltpu.VMEM((SA_WIDTH, SA_HEADS * qb), bf16),
                        pltpu.VMEM((8, qb), f32), pltpu.VMEM((8, qb), f32),
                        pltpu.VMEM((SA_WIDTH, qb), f32)],
        compiler_params=pltpu.CompilerParams(
            dimension_semantics=("parallel", "arbitrary"), vmem_limit_bytes=VMEM_LIMIT),
        name="dsa",
    )(qT, k, vT, iqT, ik, wT)


def _conv_kernel(cu_ref, halo_ref, w_ref, b_ref, lnw_ref, lnb_ref, o_ref, h_ref, *, tl, strip):
    glu = lambda u: u[:, :CV_WIDTH] * jax.nn.sigmoid(u[:, CV_WIDTH:])
    halo = glu(halo_ref[0])
    h_ref[0:CV_HALO, :] = jnp.where(pl.program_id(1) > 0, halo, 0.0)
    h_ref[CV_HALO:, :] = glu(cu_ref[0])
    lead = CV_HALO - (CV_FILTER - 1)
    for r in range(tl // strip):
        acc = jnp.zeros((strip, CV_WIDTH), f32) + b_ref[...]
        for j in range(CV_FILTER):
            acc = acc + w_ref[j:j + 1, :] * h_ref[r * strip + lead + j:r * strip + lead + j + strip, :]
        mu = jnp.mean(acc, axis=-1, keepdims=True)
        cen = acc - mu
        var = jnp.mean(cen * cen, axis=-1, keepdims=True)
        y = cen * lax.rsqrt(var + EPS) * lnw_ref[...] + lnb_ref[...]
        o_ref[0, r * strip:(r + 1) * strip, :] = _silu(y).astype(o_ref.dtype)


def _conv(cu, cv_w, cv_b, ln_w, ln_b, tl=512, strip=64):
    bsz, L, _ = cu.shape
    hb = tl // CV_HALO
    vec = pl.BlockSpec((1, CV_WIDTH), lambda b, i: (0, 0))
    return pl.pallas_call(
        functools.partial(_conv_kernel, tl=tl, strip=strip),
        grid=(bsz, L // tl),
        in_specs=[
            pl.BlockSpec((1, tl, 2 * CV_WIDTH), lambda b, i: (b, i, 0)),
            pl.BlockSpec((1, CV_HALO, 2 * CV_WIDTH), lambda b, i: (b, jnp.maximum(i * hb - 1, 0), 0)),
            pl.BlockSpec((CV_FILTER, CV_WIDTH), lambda b, i: (0, 0)),
            vec, vec, vec,
        ],
        out_specs=pl.BlockSpec((1, tl, CV_WIDTH), lambda b, i: (b, i, 0)),
        out_shape=jax.ShapeDtypeStruct((bsz, L, CV_WIDTH), bf16),
        scratch_shapes=[pltpu.VMEM((tl + CV_HALO, CV_WIDTH), f32)],
        compiler_params=pltpu.CompilerParams(dimension_semantics=("parallel", "parallel")),
        name="conv",
    )(cu, cu, cv_w, cv_b.reshape(1, -1), ln_w.reshape(1, -1), ln_b.reshape(1, -1))


def _postmix_kernel(x_ref, oa_ref, ob_ref, oc_ref, g1_ref, sh_ref, sc_ref, g2_ref, nw_ref,
                    wo_ref, w1_ref, w2_ref, fw_ref, o_ref, acc_ref, *, ff_chunk, final):
    mix = (_dot(oa_ref[0], wo_ref[0:HG_WIDTH, :])
           + _dot(ob_ref[0], wo_ref[HG_WIDTH:HG_WIDTH + SA_WIDTH, :])
           + _dot(oc_ref[0], wo_ref[HG_WIDTH + SA_WIDTH:, :]))
    x1 = x_ref[0] + g1_ref[0] * mix
    h = _modulated_norm(x1, nw_ref[...], sc_ref[0], sh_ref[0]).astype(bf16)
    for c in range(D_FF // ff_chunk):
        u = jnp.maximum(_dot(h, w1_ref[:, c * ff_chunk:(c + 1) * ff_chunk]), 0.0)
        y = _dot((u * u).astype(bf16), w2_ref[c * ff_chunk:(c + 1) * ff_chunk, :])
        if c == 0:
            acc_ref[...] = y
        else:
            acc_ref[...] += y
    x2 = x1 + g2_ref[0] * acc_ref[...]
    if final:
        ms = jnp.mean(x2 * x2, axis=-1, keepdims=True)
        x2 = x2 * lax.rsqrt(ms + EPS) * fw_ref[...]
    o_ref[0] = x2


def _postmix(x, o_a, o_b, o_c, mod, nw, wo_b, w1_b, w2_b, fw, final, tm=512, ff_chunk=1024):
    bsz, L, d = x.shape
    tok = lambda w: pl.BlockSpec((1, tm, w), lambda b, i: (b, i, 0))
    modp = lambda j: pl.BlockSpec((1, 1, d), lambda b, i: (b, 0, j))
    const = lambda shape: pl.BlockSpec(shape, lambda b, i: (0, 0), pipeline_mode=pl.Buffered(1))
    return pl.pallas_call(
        functools.partial(_postmix_kernel, ff_chunk=ff_chunk, final=final),
        grid=(bsz, L // tm),
        in_specs=[
            tok(d), tok(HG_WIDTH), tok(SA_WIDTH), tok(CV_WIDTH),
            modp(2), modp(3), modp(4), modp(5),
            pl.BlockSpec((1, d), lambda b, i: (0, 0)),
            const((d, d)), const((d, D_FF)), const((D_FF, d)),
            pl.BlockSpec((1, d), lambda b, i: (0, 0)),
        ],
        out_specs=tok(d),
        out_shape=jax.ShapeDtypeStruct((bsz, L, d), f32),
        scratch_shapes=[pltpu.VMEM((tm, d), f32)],
        compiler_params=pltpu.CompilerParams(
            dimension_semantics=("parallel", "parallel"), vmem_limit_bytes=VMEM_LIMIT),
        name="postmix",
    )(x, o_a, o_b, o_c, mod, mod, mod, mod, nw.reshape(1, d), wo_b, w1_b, w2_b, fw.reshape(1, d))


def _pack_w_in(w_in):
    pad = jnp.zeros(w_in.shape[:-1] + (D_IN_PAD - w_in.shape[-1],), w_in.dtype)
    return jnp.concatenate([w_in[..., :3072], w_in[..., 3112:], w_in[..., 3072:3112], pad], axis=-1).astype(bf16)


def _lower_bounds(lb_logits):
    p = jax.nn.softmax(lb_logits.astype(f32), axis=0)
    return jnp.cumsum(p, axis=0) - p[0]


def kernel(x, c, ada_w, ada_b, norm_mix_w, norm_mlp_w, w_in, hg_lb_logits, hg_onorm_w, cv_w, cv_b,
           cv_ln_w, cv_ln_b, w_out, mlp_w1, mlp_w2, final_norm_w):
    bsz, L, d = x.shape
    depth = w_in.shape[0]
    mods = _ada(c, ada_w, ada_b)
    lbs = _lower_bounds(hg_lb_logits)
    w_in_b = _pack_w_in(w_in)
    wo_b, w1_b, w2_b = w_out.astype(bf16), mlp_w1.astype(bf16), mlp_w2.astype(bf16)
    tr = lambda a: jnp.swapaxes(a, 1, 2)
    for l in range(depth):
        mod = mods[l].reshape(bsz, 1, N_MOD * d)
        hg, sq, sk, sv, iq, cu, ikw = _premix(x, mod, norm_mix_w[l], w_in_b[l])
        o_a = _hgrn2(hg, lbs[l], hg_onorm_w[l])
        o_b = _dsa(tr(sq), sk, tr(sv), tr(iq), ikw[..., :IDX_DIM].astype(bf16),
                   tr(ikw[..., IDX_DIM:IDX_DIM + IDX_HEADS]))
        o_c = _conv(cu, cv_w[l], cv_b[l], cv_ln_w[l], cv_ln_b[l])
        x = _postmix(x, o_a, o_b, o_c, mod, norm_mlp_w[l], wo_b[l], w1_b[l], w2_b[l],
                     final_norm_w, final=(l == depth - 1))
    return x
```

```python
import functools

import jax
import jax.numpy as jnp
from jax import lax
from jax.experimental import pallas as pl
from jax.experimental.pallas import tpu as pltpu

f32 = jnp.float32
bf16 = jnp.bfloat16

D_MODEL = 1024
DEPTH = 4
HG_HEADS = 4
HG_DK = 128
HG_DV = 128
HG_WIDTH = HG_HEADS * HG_DV
HG_CHUNK = 64
HG_SUB = 8
SA_HEADS = 4
SA_HEAD_DIM = 64
SA_WIDTH = SA_HEADS * SA_HEAD_DIM
IDX_HEADS = 8
IDX_DIM = 32
IDX_TOPK_MAX = 256
IDX_TOPK_DIV = 4
CV_WIDTH = 256
CV_FILTER = 31
CV_HALO = 32
CNT_ROWS = 64
BF16_ROWS = 16
D_FF = 4 * D_MODEL
N_MOD = 6
EPS = 1e-6
NEG_BIG = -1e30
TINY = 1e-30

IN_SIZES = (512, 512, 512, 512, 256, 256, 256, 256, 32, 8, 512)
IKW_PAD = 128
D_IN_PAD = 2048 + 768 + 256 + 512 + IKW_PAD

VMEM_LIMIT = 56 * 1024 * 1024


def _dot(a, b):
    return jnp.dot(a, b, preferred_element_type=f32)


def _dot_nt(a, b):
    return lax.dot_general(a, b, (((1,), (1,)), ((), ())), preferred_element_type=f32)


def _silu(x):
    return x * jax.nn.sigmoid(x)


def _ada_kernel(c_ref, w_ref, b_ref, o_ref):
    ca = _silu(c_ref[...])
    o_ref[0] = _dot(ca.astype(bf16), w_ref[0].astype(bf16)) + b_ref[0]


def _ada(c, ada_w, ada_b):
    depth, d, _ = ada_w.shape
    bsz = c.shape[0]
    return pl.pallas_call(
        _ada_kernel,
        grid=(depth, N_MOD),
        in_specs=[
            pl.BlockSpec((bsz, d), lambda l, j: (0, 0)),
            pl.BlockSpec((1, d, d), lambda l, j: (l, 0, j)),
            pl.BlockSpec((1, 1, d), lambda l, j: (l, 0, j)),
        ],
        out_specs=pl.BlockSpec((1, bsz, d), lambda l, j: (l, 0, j)),
        out_shape=jax.ShapeDtypeStruct((depth, bsz, N_MOD * d), f32),
        name="ada",
    )(c, ada_w, ada_b.reshape(depth, 1, N_MOD * d))


def _modulated_norm(x, nw, sc, sh):
    ms = jnp.mean(x * x, axis=-1, keepdims=True)
    y = x * lax.rsqrt(ms + EPS) * nw
    return y * (1.0 + sc) + sh


def _premix_kernel(x_ref, sh_ref, sc_ref, nw_ref, w_ref,
                   hg_ref, sq_ref, sk_ref, sv_ref, iq_ref, cu_ref, ikw_ref):
    h = _modulated_norm(x_ref[0], nw_ref[...], sc_ref[0], sh_ref[0]).astype(bf16)
    hg_ref[0] = _dot(h, w_ref[:, 0:2048])
    sq_ref[0] = _dot(h, w_ref[:, 2048:2304]).astype(bf16)
    sk_ref[0] = _dot(h, w_ref[:, 2304:2560]).astype(bf16)
    sv_ref[0] = _dot(h, w_ref[:, 2560:2816]).astype(bf16)
    iq_ref[0] = _dot(h, w_ref[:, 2816:3072]).astype(bf16)
    cu_ref[0] = _dot(h, w_ref[:, 3072:3584])
    ikw_ref[0] = _dot(h, w_ref[:, 3584:3712])


def _premix(x, mod, nw, w_in_b, tm=512):
    bsz, L, d = x.shape
    tok = lambda w: pl.BlockSpec((1, tm, w), lambda b, i: (b, i, 0))
    return pl.pallas_call(
        _premix_kernel,
        grid=(bsz, L // tm),
        in_specs=[
            tok(d),
            pl.BlockSpec((1, 1, d), lambda b, i: (b, 0, 0)),
            pl.BlockSpec((1, 1, d), lambda b, i: (b, 0, 1)),
            pl.BlockSpec((1, d), lambda b, i: (0, 0)),
            pl.BlockSpec((d, D_IN_PAD), lambda b, i: (0, 0), pipeline_mode=pl.Buffered(1)),
        ],
        out_specs=[tok(2048), tok(256), tok(256), tok(256), tok(256), tok(512), tok(IKW_PAD)],
        out_shape=[
            jax.ShapeDtypeStruct((bsz, L, 2048), f32),
            jax.ShapeDtypeStruct((bsz, L, 256), bf16),
            jax.ShapeDtypeStruct((bsz, L, 256), bf16),
            jax.ShapeDtypeStruct((bsz, L, 256), bf16),
            jax.ShapeDtypeStruct((bsz, L, 256), bf16),
            jax.ShapeDtypeStruct((bsz, L, 512), f32),
            jax.ShapeDtypeStruct((bsz, L, IKW_PAD), f32),
        ],
        compiler_params=pltpu.CompilerParams(
            dimension_semantics=("parallel", "parallel"), vmem_limit_bytes=VMEM_LIMIT),
        name="premix",
    )(x, mod, mod, nw.reshape(1, d), w_in_b)


def _split3(x):
    hi = x.astype(bf16)
    r = x - hi.astype(f32)
    mid = r.astype(bf16)
    lo = (r - mid.astype(f32)).astype(bf16)
    return hi, mid, lo


def _hgrn2_chunk(qr, fr, v, gr, lb, onw, st_ref):
    C, S, W = HG_CHUNK, HG_SUB, HG_DK
    nsub = C // S
    heads = range(HG_HEADS)
    hs = lambda x, h: x[:, h * W:(h + 1) * W]

    q = _silu(qr) * (HG_DK ** -0.5)
    f = lb + (1.0 - lb) * jax.nn.sigmoid(fr)
    lf2 = jnp.log2(jnp.maximum(f, TINY))
    k = (1.0 - lb) * jax.nn.sigmoid(-fr)
    lk2 = jnp.log2(k)

    row = lax.broadcasted_iota(jnp.int32, (C, C), 0)
    col = lax.broadcasted_iota(jnp.int32, (C, C), 1)
    tril = jnp.where(col <= row, 1.0, 0.0).astype(bf16)
    g3 = _dot(tril, jnp.concatenate(_split3(lf2), axis=1))
    nw = HG_HEADS * W
    G2 = g3[:, :nw] + g3[:, nw:2 * nw] + g3[:, 2 * nw:]
    H = lk2 - G2
    g2_last = G2[C - 1:C, :]

    lane = lax.broadcasted_iota(jnp.int32, (S, C), 1)
    srow = lax.broadcasted_iota(jnp.int32, (S, C), 0)
    ones = jnp.ones((W, C), bf16)
    zpad = lambda x: jnp.concatenate([x, jnp.zeros((C - x.shape[0], W), x.dtype)], axis=0)

    o_inter, rsums, offs = [], [], []
    for h in heads:
        qh, gh, hh, lkh = hs(q, h), hs(G2, h), hs(H, h), hs(lk2, h)
        o_inter.append(_dot_nt((qh * jnp.exp2(gh)).astype(bf16), st_ref[h].astype(bf16)))
        parts = []
        for blk in range(nsub):
            r0 = blk * S
            qb, gb = qh[r0:r0 + S], gh[r0:r0 + S]
            for j in range(r0, r0 + S):
                parts.append(qb * jnp.exp2(jnp.minimum(gb + hh[j:j + 1, :], lkh[j:j + 1, :])))
        rsums.append(_dot(jnp.concatenate(parts, axis=0).astype(bf16), ones))
        off_h = []
        for blk in range(1, nsub):
            r0 = blk * S
            gref = gh[r0 - 1:r0, :]
            qd = qh[r0:r0 + S] * jnp.exp2(gh[r0:r0 + S] - gref)
            kd = jnp.exp2(jnp.minimum(gref + hh[:r0], lkh[:r0]))
            off_h.append(_dot_nt(qd.astype(bf16), zpad(kd).astype(bf16)))
        offs.append(off_h)

    outs = []
    for h in heads:
        vh = hs(v, h)
        a_rows = []
        for blk in range(nsub):
            r0 = blk * S
            a = jnp.zeros((S, C), f32)
            for j in range(S):
                a = jnp.where(lane == r0 + j, rsums[h][(r0 + j) * S:(r0 + j + 1) * S, :], a)
            a = jnp.where(lane - r0 <= srow, a, 0.0)
            if blk > 0:
                a = a + offs[h][blk - 1]
            a_rows.append(a)
        A = jnp.concatenate(a_rows, axis=0)
        o = o_inter[h] + _dot(A.astype(bf16), vh.astype(bf16))
        kg = jnp.exp2(hs(g2_last, h) + hs(H, h))
        st_ref[h] = st_ref[h] * jnp.exp2(hs(g2_last, h)) + _dot(vh.T.astype(bf16), kg.astype(bf16))
        o = o * lax.rsqrt(jnp.mean(o * o, axis=-1, keepdims=True) + EPS)
        outs.append(o * hs(onw, h) * _silu(hs(gr, h)))
    return jnp.concatenate(outs, axis=1)


def _hgrn2_kernel(q_ref, f_ref, v_ref, g_ref, lb_ref, onw_ref, o_ref, st_ref, *, n_chunks):
    @pl.when(pl.program_id(1) == 0)
    def _():
        st_ref[...] = jnp.zeros_like(st_ref)

    lb = lb_ref[...]
    onw = onw_ref[...]

    def body(c, carry):
        r0 = pl.multiple_of(c * HG_CHUNK, HG_CHUNK)
        sl = pl.ds(r0, HG_CHUNK)
        o = _hgrn2_chunk(q_ref[0, sl, :], f_ref[0, sl, :], v_ref[0, sl, :], g_ref[0, sl, :], lb, onw, st_ref)
        o_ref[0, sl, :] = o.astype(o_ref.dtype)
        return carry

    lax.fori_loop(0, n_chunks, body, 0)


def _hgrn2(hg, lb, onw, tl=512):
    bsz, L, _ = hg.shape
    part = lambda p: pl.BlockSpec((1, tl, HG_WIDTH), lambda b, i: (b, i, p))
    vec = pl.BlockSpec((1, HG_WIDTH), lambda b, i: (0, 0))
    return pl.pallas_call(
        functools.partial(_hgrn2_kernel, n_chunks=tl // HG_CHUNK),
        grid=(bsz, L // tl),
        in_specs=[part(0), part(1), part(2), part(3), vec, vec],
        out_specs=pl.BlockSpec((1, tl, HG_WIDTH), lambda b, i: (b, i, 0)),
        out_shape=jax.ShapeDtypeStruct((bsz, L, HG_WIDTH), bf16),
        scratch_shapes=[pltpu.VMEM((HG_HEADS, HG_DV, HG_DK), f32)],
        compiler_params=pltpu.CompilerParams(
            dimension_semantics=("parallel", "arbitrary"), vmem_limit_bytes=VMEM_LIMIT),
        name="hgrn2",
    )(hg, hg, hg, hg, lb.reshape(1, -1), onw.reshape(1, -1))


def _sortable(bits):
    return bits ^ ((bits >> 31) & jnp.int32(0x7FFFFFFF))


def _dsa_kernel(qT_ref, k_ref, vT_ref, iqT_ref, ik_ref, wT_ref, o_ref, key_ref, x1_ref, dg_ref, qh_ref,
                mx_ref, den_ref, oT_ref, *, topk, qb, tk):
    i = pl.program_id(1)
    t0 = i * qb
    nk = (t0 + qb) // tk
    sub = 64
    qpos = t0 + lax.broadcasted_iota(jnp.int32, (1, qb), 1)
    hi16 = jnp.int32(-65536)

    def score_tile(kt, carry, diagonal=False):
        for r in range(tk // sub):
            s0 = pl.multiple_of(kt * tk + r * sub, sub)
            ikt = ik_ref[0, pl.ds(s0, sub), :]
            acc = jnp.zeros((sub, qb), f32)
            for h in range(IDX_HEADS):
                sh = _dot(ikt, iqT_ref[0, h * IDX_DIM:(h + 1) * IDX_DIM, :])
                acc = acc + wT_ref[0, h:h + 1, :] * jnp.maximum(sh, 0.0)
            if diagonal:
                kpos = s0 + lax.broadcasted_iota(jnp.int32, (sub, qb), 0)
                acc = jnp.where(kpos <= qpos, acc, NEG_BIG)
            bits = pltpu.bitcast(acc, jnp.int32)
            bits = jnp.where((bits & jnp.int32(0x7F800000)) == 0, 0, bits)
            key_ref[pl.ds(s0, sub), :] = _sortable(bits)
            x1_ref[pl.ds(s0, sub), :] = pltpu.bitcast(bits & hi16, f32).astype(bf16)
        return carry

    lax.fori_loop(0, nk - 1, score_tile, 0)
    score_tile(nk - 1, 0, diagonal=True)

    one, zero = jnp.ones((), bf16), jnp.zeros((), bf16)
    ngrp = CNT_ROWS // BF16_ROWS

    def count_ge(ref, cand):
        def tile(kt, acc):
            for r in range(tk // CNT_ROWS):
                s0 = pl.multiple_of(kt * tk + r * CNT_ROWS, CNT_ROWS)
                x = ref[pl.ds(s0, CNT_ROWS), :].reshape(ngrp, BF16_ROWS, qb)
                acc = acc + jnp.where(x >= cand[None], one, zero)
            return acc
        acc = lax.fori_loop(0, nk, tile, jnp.zeros((ngrp, BF16_ROWS, qb), bf16))
        return jnp.sum(acc.astype(f32).reshape(CNT_ROWS, qb), axis=0, keepdims=True)

    rows = lambda v: jnp.broadcast_to(v, (BF16_ROWS, qb))

    def prefix_cand(p):
        u = rows(jnp.where(p > 0, jnp.maximum(p, 128), p)) << 16
        return pltpu.bitcast((u ^ ((u >> 31) & jnp.int32(0x7FFFFFFF))) & hi16, f32).astype(bf16)

    def digit_cand(d):
        return rows(d).astype(f32).astype(bf16)

    def search(nbits, ref, to_cand, kth, base):
        def step(it, v):
            trial = v + (jnp.int32(1) << (nbits - 1 - it))
            return jnp.where(count_ge(ref, to_cand(trial)) >= kth, trial, v)
        return lax.fori_loop(0, nbits, step, base)

    def build_digits(match_shift, match_val, digit_shift):
        def tile(kt, carry):
            s0 = pl.multiple_of(kt * tk, tk)
            u = key_ref[pl.ds(s0, tk), :]
            d = jnp.where((u >> match_shift) == match_val, (u >> digit_shift) & 0xFF, -1)
            dg_ref[pl.ds(s0, tk), :] = d.astype(f32).astype(bf16)
            return carry
        lax.fori_loop(0, nk, tile, 0)

    zeros_i = jnp.zeros((1, qb), jnp.int32)
    p1 = search(16, x1_ref, prefix_cand, topk, jnp.full((1, qb), -2 ** 15, jnp.int32))
    k1 = topk - count_ge(x1_ref, prefix_cand(p1 + 1))
    build_digits(16, p1, 8)
    d2 = search(8, dg_ref, digit_cand, k1, zeros_i)
    k2 = k1 - count_ge(dg_ref, digit_cand(d2 + 1))
    p12 = (p1 << 8) | d2
    build_digits(8, p12, 0)
    d3 = search(8, dg_ref, digit_cand, k2, zeros_i)
    need = k2 - count_ge(dg_ref, digit_cand(d3 + 1))
    cnt_eq = count_ge(dg_ref, digit_cand(d3)) - (k2 - need)
    ustar = (p12 << 8) | d3
    take_all = qpos < topk

    def tie_limit(_):
        def step(it, p):
            trial = p | (jnp.int32(1) << (12 - it))

            def tile(kt, acc):
                s0 = pl.multiple_of(kt * tk, tk)
                kpos = s0 + lax.broadcasted_iota(jnp.int32, (tk, qb), 0)
                tied = jnp.where(kpos < trial, jnp.where(key_ref[pl.ds(s0, tk), :] == ustar, 1.0, 0.0), 0.0)
                return acc + jnp.sum(tied.reshape(tk // CNT_ROWS, CNT_ROWS, qb), axis=0)
            acc = lax.fori_loop(0, nk, tile, jnp.zeros((CNT_ROWS, qb), f32))
            return jnp.where(jnp.sum(acc, axis=0, keepdims=True) < need, trial, p)
        return lax.fori_loop(0, 13, step, zeros_i)

    has_excess = jnp.max(jnp.where(take_all, 0.0, cnt_eq - need)) > 0.0
    plim = lax.cond(has_excess, tie_limit, lambda _: jnp.full((1, qb), 2 ** 30, jnp.int32), 0)

    def bias_tile(kt, carry):
        s0 = pl.multiple_of(kt * tk, tk)
        u = key_ref[pl.ds(s0, tk), :]
        kpos = s0 + lax.broadcasted_iota(jnp.int32, (tk, qb), 0)
        tie = jnp.where(kpos <= plim, 0.0, NEG_BIG)
        bias = jnp.where(u > ustar, 0.0, jnp.where(u == ustar, tie, NEG_BIG))
        bias = jnp.where(take_all, 0.0, bias)
        key_ref[pl.ds(s0, tk), :] = pltpu.bitcast(jnp.where(kpos <= qpos, bias, NEG_BIG), jnp.int32)
        return carry

    lax.fori_loop(0, nk, bias_tile, 0)

    q_all = qT_ref[0] * jnp.asarray(SA_HEAD_DIM ** -0.5, bf16)
    hrow = lax.broadcasted_iota(jnp.int32, q_all.shape, 0) // SA_HEAD_DIM
    for h in range(SA_HEADS):
        qh_ref[:, h * qb:(h + 1) * qb] = jnp.where(hrow == h, q_all, jnp.zeros_like(q_all))
    mx_ref[...] = jnp.full(mx_ref.shape, NEG_BIG, f32)
    den_ref[...] = jnp.zeros_like(den_ref)
    oT_ref[...] = jnp.zeros_like(oT_ref)

    def att_tile(kt, carry):
        s0 = pl.multiple_of(kt * tk, tk)
        bias = pltpu.bitcast(key_ref[pl.ds(s0, tk), :], f32)
        lg_all = _dot(k_ref[0, pl.ds(s0, tk), :], qh_ref[...])
        ps, alphas = [], []
        for h in range(SA_HEADS):
            lg = lg_all[:, h * qb:(h + 1) * qb] + bias
            m_old = mx_ref[h:h + 1, :]
            m_new = jnp.maximum(m_old, jnp.max(lg, axis=0, keepdims=True))
            alpha = jnp.exp(m_old - m_new)
            p = jnp.exp(lg - m_new)
            mx_ref[h:h + 1, :] = m_new
            den_ref[h:h + 1, :] = alpha * den_ref[h:h + 1, :] + jnp.sum(p, axis=0, keepdims=True)
            ps.append(p.astype(bf16))
            alphas.append(alpha)
        for h in range(SA_HEADS):
            hs = slice(h * SA_HEAD_DIM, (h + 1) * SA_HEAD_DIM)
            oT_ref[hs, :] = alphas[h] * oT_ref[hs, :] + _dot(vT_ref[0, hs, pl.ds(s0, tk)], ps[h])
        return carry

    lax.fori_loop(0, nk, att_tile, 0)
    for h in range(SA_HEADS):
        hs = slice(h * SA_HEAD_DIM, (h + 1) * SA_HEAD_DIM)
        oT_ref[hs, :] = oT_ref[hs, :] / den_ref[h:h + 1, :]
    o_ref[0] = oT_ref[...].T.astype(o_ref.dtype)


def _dsa(qT, k, vT, iqT, ik, wT, qb=512):
    bsz, L, _ = k.shape
    topk = min(IDX_TOPK_MAX, L // IDX_TOPK_DIV)
    assert L % qb == 0 and qb >= topk and L // CNT_ROWS <= 256
    whole = lambda shape: pl.BlockSpec(shape, lambda b, i: (b, 0, 0), pipeline_mode=pl.Buffered(1))
    return pl.pallas_call(
        functools.partial(_dsa_kernel, topk=topk, qb=qb, tk=qb),
        grid=(bsz, L // qb),
        in_specs=[
            pl.BlockSpec((1, SA_WIDTH, qb), lambda b, i: (b, 0, i)),
            whole((1, L, SA_WIDTH)),
            whole((1, SA_WIDTH, L)),
            pl.BlockSpec((1, IDX_HEADS * IDX_DIM, qb), lambda b, i: (b, 0, i)),
            whole((1, L, IDX_DIM)),
            pl.BlockSpec((1, IDX_HEADS, qb), lambda b, i: (b, 0, i)),
        ],
        out_specs=pl.BlockSpec((1, qb, SA_WIDTH), lambda b, i: (b, i, 0)),
        out_shape=jax.ShapeDtypeStruct((bsz, L, SA_WIDTH), bf16),
        scratch_shapes=[pltpu.VMEM((L, qb), jnp.int32), pltpu.VMEM((L, qb), bf16), pltpu.VMEM((L, qb), bf16),
                        pltpu.VMEM((SA_WIDTH, SA_HEADS * qb), bf16),
                        pltpu.VMEM((8, qb), f32), pltpu.VMEM((8, qb), f32),
                        pltpu.VMEM((SA_WIDTH, qb), f32)],
        compiler_params=pltpu.CompilerParams(
            dimension_semantics=("parallel", "arbitrary"), vmem_limit_bytes=VMEM_LIMIT),
        name="dsa",
    )(qT, k, vT, iqT, ik, wT)


def _conv_kernel(cu_ref, halo_ref, w_ref, b_ref, lnw_ref, lnb_ref, o_ref, h_ref, *, tl, strip):
    glu = lambda u: u[:, :CV_WIDTH] * jax.nn.sigmoid(u[:, CV_WIDTH:])
    halo = glu(halo_ref[0])
    h_ref[0:CV_HALO, :] = jnp.where(pl.program_id(1) > 0, halo, 0.0)
    h_ref[CV_HALO:, :] = glu(cu_ref[0])
    lead = CV_HALO - (CV_FILTER - 1)
    for r in range(tl // strip):
        acc = jnp.zeros((strip, CV_WIDTH), f32) + b_ref[...]
        for j in range(CV_FILTER):
            acc = acc + w_ref[j:j + 1, :] * h_ref[r * strip + lead + j:r * strip + lead + j + strip, :]
        mu = jnp.mean(acc, axis=-1, keepdims=True)
        cen = acc - mu
        var = jnp.mean(cen * cen, axis=-1, keepdims=True)
        y = cen * lax.rsqrt(var + EPS) * lnw_ref[...] + lnb_ref[...]
        o_ref[0, r * strip:(r + 1) * strip, :] = _silu(y).astype(o_ref.dtype)


def _conv(cu, cv_w, cv_b, ln_w, ln_b, tl=512, strip=64):
    bsz, L, _ = cu.shape
    hb = tl // CV_HALO
    vec = pl.BlockSpec((1, CV_WIDTH), lambda b, i: (0, 0))
    return pl.pallas_call(
        functools.partial(_conv_kernel, tl=tl, strip=strip),
        grid=(bsz, L // tl),
        in_specs=[
            pl.BlockSpec((1, tl, 2 * CV_WIDTH), lambda b, i: (b, i, 0)),
            pl.BlockSpec((1, CV_HALO, 2 * CV_WIDTH), lambda b, i: (b, jnp.maximum(i * hb - 1, 0), 0)),
            pl.BlockSpec((CV_FILTER, CV_WIDTH), lambda b, i: (0, 0)),
            vec, vec, vec,
        ],
        out_specs=pl.BlockSpec((1, tl, CV_WIDTH), lambda b, i: (b, i, 0)),
        out_shape=jax.ShapeDtypeStruct((bsz, L, CV_WIDTH), bf16),
        scratch_shapes=[pltpu.VMEM((tl + CV_HALO, CV_WIDTH), f32)],
        compiler_params=pltpu.CompilerParams(dimension_semantics=("parallel", "parallel")),
        name="conv",
    )(cu, cu, cv_w, cv_b.reshape(1, -1), ln_w.reshape(1, -1), ln_b.reshape(1, -1))


def _postmix_kernel(x_ref, oa_ref, ob_ref, oc_ref, g1_ref, sh_ref, sc_ref, g2_ref, nw_ref,
                    wo_ref, w1_ref, w2_ref, fw_ref, o_ref, acc_ref, *, ff_chunk, final):
    mix = (_dot(oa_ref[0], wo_ref[0:HG_WIDTH, :])
           + _dot(ob_ref[0], wo_ref[HG_WIDTH:HG_WIDTH + SA_WIDTH, :])
           + _dot(oc_ref[0], wo_ref[HG_WIDTH + SA_WIDTH:, :]))
    x1 = x_ref[0] + g1_ref[0] * mix
    h = _modulated_norm(x1, nw_ref[...], sc_ref[0], sh_ref[0]).astype(bf16)
    for c in range(D_FF // ff_chunk):
        u = jnp.maximum(_dot(h, w1_ref[:, c * ff_chunk:(c + 1) * ff_chunk]), 0.0)
        y = _dot((u * u).astype(bf16), w2_ref[c * ff_chunk:(c + 1) * ff_chunk, :])
        if c == 0:
            acc_ref[...] = y
        else:
            acc_ref[...] += y
    x2 = x1 + g2_ref[0] * acc_ref[...]
    if final:
        ms = jnp.mean(x2 * x2, axis=-1, keepdims=True)
        x2 = x2 * lax.rsqrt(ms + EPS) * fw_ref[...]
    o_ref[0] = x2


def _postmix(x, o_a, o_b, o_c, mod, nw, wo_b, w1_b, w2_b, fw, final, tm=512, ff_chunk=1024):
    bsz, L, d = x.shape
    tok = lambda w: pl.BlockSpec((1, tm, w), lambda b, i: (b, i, 0))
    modp = lambda j: pl.BlockSpec((1, 1, d), lambda b, i: (b, 0, j))
    const = lambda shape: pl.BlockSpec(shape, lambda b, i: (0, 0), pipeline_mode=pl.Buffered(1))
    return pl.pallas_call(
        functools.partial(_postmix_kernel, ff_chunk=ff_chunk, final=final),
        grid=(bsz, L // tm),
        in_specs=[
            tok(d), tok(HG_WIDTH), tok(SA_WIDTH), tok(CV_WIDTH),
            modp(2), modp(3), modp(4), modp(5),
            pl.BlockSpec((1, d), lambda b, i: (0, 0)),
            const((d, d)), const((d, D_FF)), const((D_FF, d)),
            pl.BlockSpec((1, d), lambda b, i: (0, 0)),
        ],
        out_specs=tok(d),
        out_shape=jax.ShapeDtypeStruct((bsz, L, d), f32),
        scratch_shapes=[pltpu.VMEM((tm, d), f32)],
        compiler_params=pltpu.CompilerParams(
            dimension_semantics=("parallel", "parallel"), vmem_limit_bytes=VMEM_LIMIT),
        name="postmix",
    )(x, o_a, o_b, o_c, mod, mod, mod, mod, nw.reshape(1, d), wo_b, w1_b, w2_b, fw.reshape(1, d))


def _pack_w_in(w_in):
    pad = jnp.zeros(w_in.shape[:-1] + (D_IN_PAD - w_in.shape[-1],), w_in.dtype)
    return jnp.concatenate([w_in[..., :3072], w_in[..., 3112:], w_in[..., 3072:3112], pad], axis=-1).astype(bf16)


def _lower_bounds(lb_logits):
    p = jax.nn.softmax(lb_logits.astype(f32), axis=0)
    return jnp.cumsum(p, axis=0) - p[0]


def kernel(x, c, ada_w, ada_b, norm_mix_w, norm_mlp_w, w_in, hg_lb_logits, hg_onorm_w, cv_w, cv_b,
           cv_ln_w, cv_ln_b, w_out, mlp_w1, mlp_w2, final_norm_w):
    bsz, L, d = x.shape
    depth = w_in.shape[0]
    mods = _ada(c, ada_w, ada_b)
    lbs = _lower_bounds(hg_lb_logits)
    w_in_b = _pack_w_in(w_in)
    wo_b, w1_b, w2_b = w_out.astype(bf16), mlp_w1.astype(bf16), mlp_w2.astype(bf16)
    tr = lambda a: jnp.swapaxes(a, 1, 2)
    for l in range(depth):
        mod = mods[l].reshape(bsz, 1, N_MOD * d)
        hg, sq, sk, sv, iq, cu, ikw = _premix(x, mod, norm_mix_w[l], w_in_b[l])
        o_a = _hgrn2(hg, lbs[l], hg_onorm_w[l])
        o_b = _dsa(tr(sq), sk, tr(sv), tr(iq), ikw[..., :IDX_DIM].astype(bf16),
                   tr(ikw[..., IDX_DIM:IDX_DIM + IDX_HEADS]))
        o_c = _conv(cu, cv_w[l], cv_b[l], cv_ln_w[l], cv_ln_b[l])
        x = _postmix(x, o_a, o_b, o_c, mod, norm_mlp_w[l], wo_b[l], w1_b[l], w2_b[l],
                     final_norm_w, final=(l == depth - 1))
    return x
```

```python
import functools

import jax
import jax.numpy as jnp
from jax import lax
from jax.experimental import pallas as pl
from jax.experimental.pallas import tpu as pltpu

f32 = jnp.float32
bf16 = jnp.bfloat16

D_MODEL = 1024
DEPTH = 4
HG_HEADS = 4
HG_DK = 128
HG_DV = 128
HG_WIDTH = HG_HEADS * HG_DV
HG_CHUNK = 64
HG_SUB = 8
SA_HEADS = 4
SA_HEAD_DIM = 64
SA_WIDTH = SA_HEADS * SA_HEAD_DIM
IDX_HEADS = 8
IDX_DIM = 32
IDX_TOPK_MAX = 256
IDX_TOPK_DIV = 4
CV_WIDTH = 256
CV_FILTER = 31
CV_HALO = 32
CNT_ROWS = 64
BF16_ROWS = 16
ATT_STRIP = 64
D_FF = 4 * D_MODEL
N_MOD = 6
EPS = 1e-6
NEG_BIG = -1e30
TINY = 1e-30

IN_SIZES = (512, 512, 512, 512, 256, 256, 256, 256, 32, 8, 512)
IKW_PAD = 128
D_IN_PAD = 2048 + 768 + 256 + 512 + IKW_PAD

VMEM_LIMIT = 56 * 1024 * 1024


def _dot(a, b):
    return jnp.dot(a, b, preferred_element_type=f32)


def _dot_nt(a, b):
    return lax.dot_general(a, b, (((1,), (1,)), ((), ())), preferred_element_type=f32)


def _silu(x):
    return x * jax.nn.sigmoid(x)


def _ada_kernel(c_ref, w_ref, b_ref, o_ref):
    ca = _silu(c_ref[...])
    o_ref[0] = _dot(ca.astype(bf16), w_ref[0].astype(bf16)) + b_ref[0]


def _ada(c, ada_w, ada_b):
    depth, d, _ = ada_w.shape
    bsz = c.shape[0]
    return pl.pallas_call(
        _ada_kernel,
        grid=(depth, N_MOD),
        in_specs=[
            pl.BlockSpec((bsz, d), lambda l, j: (0, 0)),
            pl.BlockSpec((1, d, d), lambda l, j: (l, 0, j)),
            pl.BlockSpec((1, 1, d), lambda l, j: (l, 0, j)),
        ],
        out_specs=pl.BlockSpec((1, bsz, d), lambda l, j: (l, 0, j)),
        out_shape=jax.ShapeDtypeStruct((depth, bsz, N_MOD * d), f32),
        name="ada",
    )(c, ada_w, ada_b.reshape(depth, 1, N_MOD * d))


def _modulated_norm(x, nw, sc, sh):
    ms = jnp.mean(x * x, axis=-1, keepdims=True)
    y = x * lax.rsqrt(ms + EPS) * nw
    return y * (1.0 + sc) + sh


def _premix_kernel(x_ref, sh_ref, sc_ref, nw_ref, w_ref,
                   hg_ref, sq_ref, sk_ref, sv_ref, iq_ref, cu_ref, ikw_ref):
    h = _modulated_norm(x_ref[0], nw_ref[...], sc_ref[0], sh_ref[0]).astype(bf16)
    hg_ref[0] = _dot(h, w_ref[:, 0:2048])
    sq_ref[0] = _dot(h, w_ref[:, 2048:2304]).astype(bf16)
    sk_ref[0] = _dot(h, w_ref[:, 2304:2560]).astype(bf16)
    sv_ref[0] = _dot(h, w_ref[:, 2560:2816]).astype(bf16)
    iq_ref[0] = _dot(h, w_ref[:, 2816:3072]).astype(bf16)
    cu_ref[0] = _dot(h, w_ref[:, 3072:3584])
    ikw_ref[0] = _dot(h, w_ref[:, 3584:3712])


def _premix(x, mod, nw, w_in_b, tm=512):
    bsz, L, d = x.shape
    tok = lambda w: pl.BlockSpec((1, tm, w), lambda b, i: (b, i, 0))
    return pl.pallas_call(
        _premix_kernel,
        grid=(bsz, L // tm),
        in_specs=[
            tok(d),
            pl.BlockSpec((1, 1, d), lambda b, i: (b, 0, 0)),
            pl.BlockSpec((1, 1, d), lambda b, i: (b, 0, 1)),
            pl.BlockSpec((1, d), lambda b, i: (0, 0)),
            pl.BlockSpec((d, D_IN_PAD), lambda b, i: (0, 0), pipeline_mode=pl.Buffered(1)),
        ],
        out_specs=[tok(2048), tok(256), tok(256), tok(256), tok(256), tok(512), tok(IKW_PAD)],
        out_shape=[
            jax.ShapeDtypeStruct((bsz, L, 2048), f32),
            jax.ShapeDtypeStruct((bsz, L, 256), bf16),
            jax.ShapeDtypeStruct((bsz, L, 256), bf16),
            jax.ShapeDtypeStruct((bsz, L, 256), bf16),
            jax.ShapeDtypeStruct((bsz, L, 256), bf16),
            jax.ShapeDtypeStruct((bsz, L, 512), f32),
            jax.ShapeDtypeStruct((bsz, L, IKW_PAD), f32),
        ],
        compiler_params=pltpu.CompilerParams(
            dimension_semantics=("parallel", "parallel"), vmem_limit_bytes=VMEM_LIMIT),
        name="premix",
    )(x, mod, mod, nw.reshape(1, d), w_in_b)


def _split3(x):
    hi = x.astype(bf16)
    r = x - hi.astype(f32)
    mid = r.astype(bf16)
    lo = (r - mid.astype(f32)).astype(bf16)
    return hi, mid, lo


def _hgrn2_chunk(qr, fr, v, gr, lb, onw, st_ref):
    C, S, W = HG_CHUNK, HG_SUB, HG_DK
    nsub = C // S
    heads = range(HG_HEADS)
    hs = lambda x, h: x[:, h * W:(h + 1) * W]

    q = _silu(qr) * (HG_DK ** -0.5)
    f = lb + (1.0 - lb) * jax.nn.sigmoid(fr)
    lf2 = jnp.log2(jnp.maximum(f, TINY))
    k = (1.0 - lb) * jax.nn.sigmoid(-fr)
    lk2 = jnp.log2(k)

    row = lax.broadcasted_iota(jnp.int32, (C, C), 0)
    col = lax.broadcasted_iota(jnp.int32, (C, C), 1)
    tril = jnp.where(col <= row, 1.0, 0.0).astype(bf16)
    g3 = _dot(tril, jnp.concatenate(_split3(lf2), axis=1))
    nw = HG_HEADS * W
    G2 = g3[:, :nw] + g3[:, nw:2 * nw] + g3[:, 2 * nw:]
    H = lk2 - G2
    g2_last = G2[C - 1:C, :]

    lane = lax.broadcasted_iota(jnp.int32, (S, C), 1)
    srow = lax.broadcasted_iota(jnp.int32, (S, C), 0)
    ones = jnp.ones((W, C), bf16)
    zpad = lambda x: jnp.concatenate([x, jnp.zeros((C - x.shape[0], W), x.dtype)], axis=0)

    o_inter, rsums, offs = [], [], []
    for h in heads:
        qh, gh, hh, lkh = hs(q, h), hs(G2, h), hs(H, h), hs(lk2, h)
        o_inter.append(_dot_nt((qh * jnp.exp2(gh)).astype(bf16), st_ref[h].astype(bf16)))
        parts = []
        for blk in range(nsub):
            r0 = blk * S
            qb, gb = qh[r0:r0 + S], gh[r0:r0 + S]
            for j in range(r0, r0 + S):
                parts.append(qb * jnp.exp2(jnp.minimum(gb + hh[j:j + 1, :], lkh[j:j + 1, :])))
        rsums.append(_dot(jnp.concatenate(parts, axis=0).astype(bf16), ones))
        off_h = []
        for blk in range(1, nsub):
            r0 = blk * S
            gref = gh[r0 - 1:r0, :]
            qd = qh[r0:r0 + S] * jnp.exp2(gh[r0:r0 + S] - gref)
            kd = jnp.exp2(jnp.minimum(gref + hh[:r0], lkh[:r0]))
            off_h.append(_dot_nt(qd.astype(bf16), zpad(kd).astype(bf16)))
        offs.append(off_h)

    outs = []
    for h in heads:
        vh = hs(v, h)
        a_rows = []
        for blk in range(nsub):
            r0 = blk * S
            a = jnp.zeros((S, C), f32)
            for j in range(S):
                a = jnp.where(lane == r0 + j, rsums[h][(r0 + j) * S:(r0 + j + 1) * S, :], a)
            a = jnp.where(lane - r0 <= srow, a, 0.0)
            if blk > 0:
                a = a + offs[h][blk - 1]
            a_rows.append(a)
        A = jnp.concatenate(a_rows, axis=0)
        o = o_inter[h] + _dot(A.astype(bf16), vh.astype(bf16))
        kg = jnp.exp2(hs(g2_last, h) + hs(H, h))
        st_ref[h] = st_ref[h] * jnp.exp2(hs(g2_last, h)) + _dot(vh.T.astype(bf16), kg.astype(bf16))
        o = o * lax.rsqrt(jnp.mean(o * o, axis=-1, keepdims=True) + EPS)
        outs.append(o * hs(onw, h) * _silu(hs(gr, h)))
    return jnp.concatenate(outs, axis=1)


def _hgrn2_kernel(q_ref, f_ref, v_ref, g_ref, lb_ref, onw_ref, o_ref, st_ref, *, n_chunks):
    @pl.when(pl.program_id(1) == 0)
    def _():
        st_ref[...] = jnp.zeros_like(st_ref)

    lb = lb_ref[...]
    onw = onw_ref[...]

    def body(c, carry):
        r0 = pl.multiple_of(c * HG_CHUNK, HG_CHUNK)
        sl = pl.ds(r0, HG_CHUNK)
        o = _hgrn2_chunk(q_ref[0, sl, :], f_ref[0, sl, :], v_ref[0, sl, :], g_ref[0, sl, :], lb, onw, st_ref)
        o_ref[0, sl, :] = o.astype(o_ref.dtype)
        return carry

    lax.fori_loop(0, n_chunks, body, 0)


def _hgrn2(hg, lb, onw, tl=512):
    bsz, L, _ = hg.shape
    part = lambda p: pl.BlockSpec((1, tl, HG_WIDTH), lambda b, i: (b, i, p))
    vec = pl.BlockSpec((1, HG_WIDTH), lambda b, i: (0, 0))
    return pl.pallas_call(
        functools.partial(_hgrn2_kernel, n_chunks=tl // HG_CHUNK),
        grid=(bsz, L // tl),
        in_specs=[part(0), part(1), part(2), part(3), vec, vec],
        out_specs=pl.BlockSpec((1, tl, HG_WIDTH), lambda b, i: (b, i, 0)),
        out_shape=jax.ShapeDtypeStruct((bsz, L, HG_WIDTH), bf16),
        scratch_shapes=[pltpu.VMEM((HG_HEADS, HG_DV, HG_DK), f32)],
        compiler_params=pltpu.CompilerParams(
            dimension_semantics=("parallel", "arbitrary"), vmem_limit_bytes=VMEM_LIMIT),
        name="hgrn2",
    )(hg, hg, hg, hg, lb.reshape(1, -1), onw.reshape(1, -1))


def _sortable(bits):
    return bits ^ ((bits >> 31) & jnp.int32(0x7FFFFFFF))


def _dsa_kernel(qT_ref, k_ref, vT_ref, iqT_ref, ik_ref, wT_ref, o_ref, key_ref, x1_ref, dg_ref, qh_ref,
                mx_ref, den_ref, oT_ref, lga_ref, lgb_ref, p_ref, *, topk, qb, tk):
    i = pl.program_id(1)
    t0 = i * qb
    nk = (t0 + qb) // tk
    sub = 64
    qpos = t0 + lax.broadcasted_iota(jnp.int32, (1, qb), 1)
    hi16 = jnp.int32(-65536)

    def score_tile(kt, carry, diagonal=False):
        for r in range(tk // sub):
            s0 = pl.multiple_of(kt * tk + r * sub, sub)
            ikt = ik_ref[0, pl.ds(s0, sub), :]
            acc = jnp.zeros((sub, qb), f32)
            for h in range(IDX_HEADS):
                sh = _dot(ikt, iqT_ref[0, h * IDX_DIM:(h + 1) * IDX_DIM, :])
                acc = acc + wT_ref[0, h:h + 1, :] * jnp.maximum(sh, 0.0)
            if diagonal:
                kpos = s0 + lax.broadcasted_iota(jnp.int32, (sub, qb), 0)
                acc = jnp.where(kpos <= qpos, acc, NEG_BIG)
            bits = pltpu.bitcast(acc, jnp.int32)
            bits = jnp.where((bits & jnp.int32(0x7F800000)) == 0, 0, bits)
            key_ref[pl.ds(s0, sub), :] = _sortable(bits)
            x1_ref[pl.ds(s0, sub), :] = pltpu.bitcast(bits & hi16, f32).astype(bf16)
        return carry

    lax.fori_loop(0, nk - 1, score_tile, 0)
    score_tile(nk - 1, 0, diagonal=True)

    one, zero = jnp.ones((), bf16), jnp.zeros((), bf16)
    ngrp = CNT_ROWS // BF16_ROWS

    def count_ge(ref, cand):
        def tile(kt, acc):
            for r in range(tk // CNT_ROWS):
                s0 = pl.multiple_of(kt * tk + r * CNT_ROWS, CNT_ROWS)
                x = ref[pl.ds(s0, CNT_ROWS), :].reshape(ngrp, BF16_ROWS, qb)
                acc = acc + jnp.where(x >= cand[None], one, zero)
            return acc
        acc = lax.fori_loop(0, nk, tile, jnp.zeros((ngrp, BF16_ROWS, qb), bf16))
        return jnp.sum(acc.astype(f32).reshape(CNT_ROWS, qb), axis=0, keepdims=True)

    rows = lambda v: jnp.broadcast_to(v, (BF16_ROWS, qb))

    def prefix_cand(p):
        u = rows(jnp.where(p > 0, jnp.maximum(p, 128), p)) << 16
        return pltpu.bitcast((u ^ ((u >> 31) & jnp.int32(0x7FFFFFFF))) & hi16, f32).astype(bf16)

    def digit_cand(d):
        return rows(d).astype(f32).astype(bf16)

    def search(nbits, ref, to_cand, kth, base):
        def step(it, v):
            trial = v + (jnp.int32(1) << (nbits - 1 - it))
            return jnp.where(count_ge(ref, to_cand(trial)) >= kth, trial, v)
        return lax.fori_loop(0, nbits, step, base)

    def build_digits(match_shift, match_val, digit_shift):
        def tile(kt, carry):
            s0 = pl.multiple_of(kt * tk, tk)
            u = key_ref[pl.ds(s0, tk), :]
            d = jnp.where((u >> match_shift) == match_val, (u >> digit_shift) & 0xFF, -1)
            dg_ref[pl.ds(s0, tk), :] = d.astype(f32).astype(bf16)
            return carry
        lax.fori_loop(0, nk, tile, 0)

    zeros_i = jnp.zeros((1, qb), jnp.int32)
    p1 = search(16, x1_ref, prefix_cand, topk, jnp.full((1, qb), -2 ** 15, jnp.int32))
    k1 = topk - count_ge(x1_ref, prefix_cand(p1 + 1))
    build_digits(16, p1, 8)
    d2 = search(8, dg_ref, digit_cand, k1, zeros_i)
    k2 = k1 - count_ge(dg_ref, digit_cand(d2 + 1))
    in_p12 = count_ge(dg_ref, digit_cand(d2)) - (k1 - k2)
    p12 = (p1 << 8) | d2
    take_all = qpos < topk

    def last_digit(_):
        build_digits(8, p12, 0)
        d3 = search(8, dg_ref, digit_cand, k2, zeros_i)
        need = k2 - count_ge(dg_ref, digit_cand(d3 + 1))
        cnt_eq = count_ge(dg_ref, digit_cand(d3)) - (k2 - need)
        return (p12 << 8) | d3, need, cnt_eq

    def whole_prefix(_):
        return p12 << 8, k2, k2

    prefix_taken = jnp.max(jnp.where(take_all, 0.0, in_p12 - k2)) <= 0.0
    ustar, need, cnt_eq = lax.cond(prefix_taken, whole_prefix, last_digit, 0)

    def tie_limit(_):
        def step(it, p):
            trial = p | (jnp.int32(1) << (12 - it))

            def tile(kt, acc):
                s0 = pl.multiple_of(kt * tk, tk)
                kpos = s0 + lax.broadcasted_iota(jnp.int32, (tk, qb), 0)
                tied = jnp.where(kpos < trial, jnp.where(key_ref[pl.ds(s0, tk), :] == ustar, 1.0, 0.0), 0.0)
                return acc + jnp.sum(tied.reshape(tk // CNT_ROWS, CNT_ROWS, qb), axis=0)
            acc = lax.fori_loop(0, nk, tile, jnp.zeros((CNT_ROWS, qb), f32))
            return jnp.where(jnp.sum(acc, axis=0, keepdims=True) < need, trial, p)
        return lax.fori_loop(0, 13, step, zeros_i)

    has_excess = jnp.max(jnp.where(take_all, 0.0, cnt_eq - need)) > 0.0
    plim = lax.cond(has_excess, tie_limit, lambda _: jnp.full((1, qb), 2 ** 30, jnp.int32), 0)

    def bias_tile(kt, carry):
        s0 = pl.multiple_of(kt * tk, tk)
        u = key_ref[pl.ds(s0, tk), :]
        kpos = s0 + lax.broadcasted_iota(jnp.int32, (tk, qb), 0)
        tie = jnp.where(kpos <= plim, 0.0, NEG_BIG)
        bias = jnp.where(u > ustar, 0.0, jnp.where(u == ustar, tie, NEG_BIG))
        bias = jnp.where(take_all, 0.0, bias)
        key_ref[pl.ds(s0, tk), :] = pltpu.bitcast(jnp.where(kpos <= qpos, bias, NEG_BIG), jnp.int32)
        return carry

    lax.fori_loop(0, nk, bias_tile, 0)

    q_all = qT_ref[0] * jnp.asarray(SA_HEAD_DIM ** -0.5, bf16)
    hrow = lax.broadcasted_iota(jnp.int32, q_all.shape, 0) // SA_HEAD_DIM
    for h in range(SA_HEADS):
        qh_ref[:, h * qb:(h + 1) * qb] = jnp.where(hrow == h, q_all, jnp.zeros_like(q_all))
    mx_ref[...] = jnp.full(mx_ref.shape, NEG_BIG, f32)
    den_ref[...] = jnp.zeros_like(den_ref)
    oT_ref[...] = jnp.zeros_like(oT_ref)

    ta = tk // 2
    n_att = 2 * nk

    def logits(t, dst):
        s0 = pl.multiple_of(t * ta, ta)
        dst[...] = _dot(k_ref[0, pl.ds(s0, ta), :], qh_ref[...])

    def softmax_pv(t, src):
        s0 = pl.multiple_of(t * ta, ta)
        alphas = []
        for h in range(SA_HEADS):
            cols = slice(h * qb, (h + 1) * qb)
            mt = jnp.full((8, qb), NEG_BIG, f32)
            for r in range(0, ta, ATT_STRIP):
                lg = src[r:r + ATT_STRIP, cols] + pltpu.bitcast(key_ref[pl.ds(s0 + r, ATT_STRIP), :], f32)
                src[r:r + ATT_STRIP, cols] = lg
                mt = jnp.maximum(mt, jnp.max(lg.reshape(ATT_STRIP // 8, 8, qb), axis=0))
            m_old = mx_ref[h:h + 1, :]
            m_new = jnp.maximum(m_old, jnp.max(mt, axis=0, keepdims=True))
            alphas.append(jnp.exp(m_old - m_new))
            mx_ref[h:h + 1, :] = m_new
            dsum = jnp.zeros((8, qb), f32)
            for r in range(0, ta, ATT_STRIP):
                p = jnp.exp(src[r:r + ATT_STRIP, cols] - m_new)
                dsum = dsum + jnp.sum(p.reshape(ATT_STRIP // 8, 8, qb), axis=0)
                p_ref[h, r:r + ATT_STRIP, :] = p.astype(bf16)
            den_ref[h:h + 1, :] = alphas[h] * den_ref[h:h + 1, :] + jnp.sum(dsum, axis=0, keepdims=True)
        for h in range(SA_HEADS):
            hs = slice(h * SA_HEAD_DIM, (h + 1) * SA_HEAD_DIM)
            oT_ref[hs, :] = alphas[h] * oT_ref[hs, :] + _dot(vT_ref[0, hs, pl.ds(s0, ta)], p_ref[h])

    logits(0, lga_ref)

    def att_pair(j, carry):
        logits(2 * j + 1, lgb_ref)
        softmax_pv(2 * j, lga_ref)
        logits(jnp.minimum(2 * j + 2, n_att - 1), lga_ref)
        softmax_pv(2 * j + 1, lgb_ref)
        return carry

    lax.fori_loop(0, nk, att_pair, 0)
    for h in range(SA_HEADS):
        hs = slice(h * SA_HEAD_DIM, (h + 1) * SA_HEAD_DIM)
        oT_ref[hs, :] = oT_ref[hs, :] / den_ref[h:h + 1, :]
    o_ref[0] = oT_ref[...].T.astype(o_ref.dtype)


def _dsa(qT, k, vT, iqT, ik, wT, qb=512):
    bsz, L, _ = k.shape
    topk = min(IDX_TOPK_MAX, L // IDX_TOPK_DIV)
    assert L % qb == 0 and qb >= topk and L // CNT_ROWS <= 256
    whole = lambda shape: pl.BlockSpec(shape, lambda b, i: (b, 0, 0), pipeline_mode=pl.Buffered(1))
    return pl.pallas_call(
        functools.partial(_dsa_kernel, topk=topk, qb=qb, tk=qb),
        grid=(bsz, L // qb),
        in_specs=[
            pl.BlockSpec((1, SA_WIDTH, qb), lambda b, i: (b, 0, i)),
            whole((1, L, SA_WIDTH)),
            whole((1, SA_WIDTH, L)),
            pl.BlockSpec((1, IDX_HEADS * IDX_DIM, qb), lambda b, i: (b, 0, i)),
            whole((1, L, IDX_DIM)),
            pl.BlockSpec((1, IDX_HEADS, qb), lambda b, i: (b, 0, i)),
        ],
        out_specs=pl.BlockSpec((1, qb, SA_WIDTH), lambda b, i: (b, i, 0)),
        out_shape=jax.ShapeDtypeStruct((bsz, L, SA_WIDTH), bf16),
        scratch_shapes=[pltpu.VMEM((L, qb), jnp.int32), pltpu.VMEM((L, qb), bf16), pltpu.VMEM((L, qb), bf16),
                        pltpu.VMEM((SA_WIDTH, SA_HEADS * qb), bf16),
                        pltpu.VMEM((8, qb), f32), pltpu.VMEM((8, qb), f32),
                        pltpu.VMEM((SA_WIDTH, qb), f32),
                        pltpu.VMEM((qb // 2, SA_HEADS * qb), f32), pltpu.VMEM((qb // 2, SA_HEADS * qb), f32),
                        pltpu.VMEM((SA_HEADS, qb // 2, qb), bf16)],
        compiler_params=pltpu.CompilerParams(
            dimension_semantics=("parallel", "arbitrary"), vmem_limit_bytes=VMEM_LIMIT),
        name="dsa",
    )(qT, k, vT, iqT, ik, wT)


def _conv_kernel(cu_ref, halo_ref, w_ref, b_ref, lnw_ref, lnb_ref, o_ref, h_ref, *, tl, strip):
    glu = lambda u: u[:, :CV_WIDTH] * jax.nn.sigmoid(u[:, CV_WIDTH:])
    halo = glu(halo_ref[0])
    h_ref[0, 0:CV_HALO, :] = jnp.where(pl.program_id(1) > 0, halo, 0.0)
    h_ref[0, CV_HALO:, :] = glu(cu_ref[0])
    for s in range(1, 8):
        h_ref[s, 0:tl + CV_HALO - 8, :] = h_ref[0, s:s + tl + CV_HALO - 8, :]
    lead = CV_HALO - (CV_FILTER - 1)
    for r in range(tl // strip):
        acc = jnp.zeros((strip, CV_WIDTH), f32) + b_ref[...]
        for j in range(CV_FILTER):
            s, base = (lead + j) % 8, (lead + j) // 8 * 8
            acc = acc + w_ref[j:j + 1, :] * h_ref[s, r * strip + base:r * strip + base + strip, :]
        mu = jnp.mean(acc, axis=-1, keepdims=True)
        cen = acc - mu
        var = jnp.mean(cen * cen, axis=-1, keepdims=True)
        y = cen * lax.rsqrt(var + EPS) * lnw_ref[...] + lnb_ref[...]
        o_ref[0, r * strip:(r + 1) * strip, :] = _silu(y).astype(o_ref.dtype)


def _conv(cu, cv_w, cv_b, ln_w, ln_b, tl=512, strip=64):
    bsz, L, _ = cu.shape
    hb = tl // CV_HALO
    vec = pl.BlockSpec((1, CV_WIDTH), lambda b, i: (0, 0))
    return pl.pallas_call(
        functools.partial(_conv_kernel, tl=tl, strip=strip),
        grid=(bsz, L // tl),
        in_specs=[
            pl.BlockSpec((1, tl, 2 * CV_WIDTH), lambda b, i: (b, i, 0)),
            pl.BlockSpec((1, CV_HALO, 2 * CV_WIDTH), lambda b, i: (b, jnp.maximum(i * hb - 1, 0), 0)),
            pl.BlockSpec((CV_FILTER, CV_WIDTH), lambda b, i: (0, 0)),
            vec, vec, vec,
        ],
        out_specs=pl.BlockSpec((1, tl, CV_WIDTH), lambda b, i: (b, i, 0)),
        out_shape=jax.ShapeDtypeStruct((bsz, L, CV_WIDTH), bf16),
        scratch_shapes=[pltpu.VMEM((8, tl + CV_HALO, CV_WIDTH), f32)],
        compiler_params=pltpu.CompilerParams(dimension_semantics=("parallel", "parallel")),
        name="conv",
    )(cu, cu, cv_w, cv_b.reshape(1, -1), ln_w.reshape(1, -1), ln_b.reshape(1, -1))


def _postmix_kernel(x_ref, oa_ref, ob_ref, oc_ref, g1_ref, sh_ref, sc_ref, g2_ref, nw_ref,
                    wo_ref, w1_ref, w2_ref, fw_ref, o_ref, acc_ref, *, ff_chunk, final):
    mix = (_dot(oa_ref[0], wo_ref[0:HG_WIDTH, :])
           + _dot(ob_ref[0], wo_ref[HG_WIDTH:HG_WIDTH + SA_WIDTH, :])
           + _dot(oc_ref[0], wo_ref[HG_WIDTH + SA_WIDTH:, :]))
    x1 = x_ref[0] + g1_ref[0] * mix
    h = _modulated_norm(x1, nw_ref[...], sc_ref[0], sh_ref[0]).astype(bf16)
    for c in range(D_FF // ff_chunk):
        u = jnp.maximum(_dot(h, w1_ref[:, c * ff_chunk:(c + 1) * ff_chunk]), 0.0)
        y = _dot((u * u).astype(bf16), w2_ref[c * ff_chunk:(c + 1) * ff_chunk, :])
        if c == 0:
            acc_ref[...] = y
        else:
            acc_ref[...] += y
    x2 = x1 + g2_ref[0] * acc_ref[...]
    if final:
        ms = jnp.mean(x2 * x2, axis=-1, keepdims=True)
        x2 = x2 * lax.rsqrt(ms + EPS) * fw_ref[...]
    o_ref[0] = x2


def _postmix(x, o_a, o_b, o_c, mod, nw, wo_b, w1_b, w2_b, fw, final, tm=512, ff_chunk=1024):
    bsz, L, d = x.shape
    tok = lambda w: pl.BlockSpec((1, tm, w), lambda b, i: (b, i, 0))
    modp = lambda j: pl.BlockSpec((1, 1, d), lambda b, i: (b, 0, j))
    const = lambda shape: pl.BlockSpec(shape, lambda b, i: (0, 0), pipeline_mode=pl.Buffered(1))
    return pl.pallas_call(
        functools.partial(_postmix_kernel, ff_chunk=ff_chunk, final=final),
        grid=(bsz, L // tm),
        in_specs=[
            tok(d), tok(HG_WIDTH), tok(SA_WIDTH), tok(CV_WIDTH),
            modp(2), modp(3), modp(4), modp(5),
            pl.BlockSpec((1, d), lambda b, i: (0, 0)),
            const((d, d)), const((d, D_FF)), const((D_FF, d)),
            pl.BlockSpec((1, d), lambda b, i: (0, 0)),
        ],
        out_specs=tok(d),
        out_shape=jax.ShapeDtypeStruct((bsz, L, d), f32),
        scratch_shapes=[pltpu.VMEM((tm, d), f32)],
        compiler_params=pltpu.CompilerParams(
            dimension_semantics=("parallel", "parallel"), vmem_limit_bytes=VMEM_LIMIT),
        name="postmix",
    )(x, o_a, o_b, o_c, mod, mod, mod, mod, nw.reshape(1, d), wo_b, w1_b, w2_b, fw.reshape(1, d))


def _pack_w_in(w_in):
    pad = jnp.zeros(w_in.shape[:-1] + (D_IN_PAD - w_in.shape[-1],), w_in.dtype)
    return jnp.concatenate([w_in[..., :3072], w_in[..., 3112:], w_in[..., 3072:3112], pad], axis=-1).astype(bf16)


def _lower_bounds(lb_logits):
    p = jax.nn.softmax(lb_logits.astype(f32), axis=0)
    return jnp.cumsum(p, axis=0) - p[0]


def kernel(x, c, ada_w, ada_b, norm_mix_w, norm_mlp_w, w_in, hg_lb_logits, hg_onorm_w, cv_w, cv_b,
           cv_ln_w, cv_ln_b, w_out, mlp_w1, mlp_w2, final_norm_w):
    bsz, L, d = x.shape
    depth = w_in.shape[0]
    mods = _ada(c, ada_w, ada_b)
    lbs = _lower_bounds(hg_lb_logits)
    w_in_b = _pack_w_in(w_in)
    wo_b, w1_b, w2_b = w_out.astype(bf16), mlp_w1.astype(bf16), mlp_w2.astype(bf16)
    tr = lambda a: jnp.swapaxes(a, 1, 2)
    for l in range(depth):
        mod = mods[l].reshape(bsz, 1, N_MOD * d)
        hg, sq, sk, sv, iq, cu, ikw = _premix(x, mod, norm_mix_w[l], w_in_b[l])
        o_a = _hgrn2(hg, lbs[l], hg_onorm_w[l])
        o_b = _dsa(tr(sq), sk, tr(sv), tr(iq), ikw[..., :IDX_DIM].astype(bf16),
                   tr(ikw[..., IDX_DIM:IDX_DIM + IDX_HEADS]))
        o_c = _conv(cu, cv_w[l], cv_b[l], cv_ln_w[l], cv_ln_b[l])
        x = _postmix(x, o_a, o_b, o_c, mod, norm_mlp_w[l], wo_b[l], w1_b[l], w2_b[l],
                     final_norm_w, final=(l == depth - 1))
    return x
```

```python
import functools

import jax
import jax.numpy as jnp
from jax import lax
from jax.experimental import pallas as pl
from jax.experimental.pallas import tpu as pltpu

f32 = jnp.float32
bf16 = jnp.bfloat16

D_MODEL = 1024
DEPTH = 4
HG_HEADS = 4
HG_DK = 128
HG_DV = 128
HG_WIDTH = HG_HEADS * HG_DV
HG_CHUNK = 64
HG_SUB = 8
SA_HEADS = 4
SA_HEAD_DIM = 64
SA_WIDTH = SA_HEADS * SA_HEAD_DIM
IDX_HEADS = 8
IDX_DIM = 32
IDX_TOPK_MAX = 256
IDX_TOPK_DIV = 4
CV_WIDTH = 256
CV_FILTER = 31
CV_HALO = 32
CNT_ROWS = 32
ATT_STRIP = 64
D_FF = 4 * D_MODEL
N_MOD = 6
EPS = 1e-6
NEG_BIG = -1e30
TINY = 1e-30

IN_SIZES = (512, 512, 512, 512, 256, 256, 256, 256, 32, 8, 512)
IKW_PAD = 128
D_IN_PAD = 2048 + 768 + 256 + 512 + IKW_PAD

VMEM_LIMIT = 56 * 1024 * 1024


def _dot(a, b):
    return jnp.dot(a, b, preferred_element_type=f32)


def _dot_nt(a, b):
    return lax.dot_general(a, b, (((1,), (1,)), ((), ())), preferred_element_type=f32)


def _silu(x):
    return x * jax.nn.sigmoid(x)


def _ada_kernel(c_ref, w_ref, b_ref, o_ref):
    ca = _silu(c_ref[...])
    o_ref[0] = _dot(ca.astype(bf16), w_ref[0].astype(bf16)) + b_ref[0]


def _ada(c, ada_w, ada_b):
    depth, d, _ = ada_w.shape
    bsz = c.shape[0]
    return pl.pallas_call(
        _ada_kernel,
        grid=(depth, N_MOD),
        in_specs=[
            pl.BlockSpec((bsz, d), lambda l, j: (0, 0)),
            pl.BlockSpec((1, d, d), lambda l, j: (l, 0, j)),
            pl.BlockSpec((1, 1, d), lambda l, j: (l, 0, j)),
        ],
        out_specs=pl.BlockSpec((1, bsz, d), lambda l, j: (l, 0, j)),
        out_shape=jax.ShapeDtypeStruct((depth, bsz, N_MOD * d), f32),
        name="ada",
    )(c, ada_w, ada_b.reshape(depth, 1, N_MOD * d))


def _modulated_norm(x, nw, sc, sh):
    ms = jnp.mean(x * x, axis=-1, keepdims=True)
    y = x * lax.rsqrt(ms + EPS) * nw
    return y * (1.0 + sc) + sh


def _premix_kernel(x_ref, sh_ref, sc_ref, nw_ref, w_ref,
                   hg_ref, sq_ref, sk_ref, sv_ref, iq_ref, cu_ref, ikw_ref):
    h = _modulated_norm(x_ref[0], nw_ref[...], sc_ref[0], sh_ref[0]).astype(bf16)
    hg_ref[0] = _dot(h, w_ref[:, 0:2048])
    sq_ref[0] = _dot(h, w_ref[:, 2048:2304]).astype(bf16)
    sk_ref[0] = _dot(h, w_ref[:, 2304:2560]).astype(bf16)
    sv_ref[0] = _dot(h, w_ref[:, 2560:2816]).astype(bf16)
    iq_ref[0] = _dot(h, w_ref[:, 2816:3072]).astype(bf16)
    cu_ref[0] = _dot(h, w_ref[:, 3072:3584])
    ikw_ref[0] = _dot(h, w_ref[:, 3584:3712])


def _premix(x, mod, nw, w_in_b, tm=512):
    bsz, L, d = x.shape
    tok = lambda w: pl.BlockSpec((1, tm, w), lambda b, i: (b, i, 0))
    return pl.pallas_call(
        _premix_kernel,
        grid=(bsz, L // tm),
        in_specs=[
            tok(d),
            pl.BlockSpec((1, 1, d), lambda b, i: (b, 0, 0)),
            pl.BlockSpec((1, 1, d), lambda b, i: (b, 0, 1)),
            pl.BlockSpec((1, d), lambda b, i: (0, 0)),
            pl.BlockSpec((d, D_IN_PAD), lambda b, i: (0, 0), pipeline_mode=pl.Buffered(1)),
        ],
        out_specs=[tok(2048), tok(256), tok(256), tok(256), tok(256), tok(512), tok(IKW_PAD)],
        out_shape=[
            jax.ShapeDtypeStruct((bsz, L, 2048), f32),
            jax.ShapeDtypeStruct((bsz, L, 256), bf16),
            jax.ShapeDtypeStruct((bsz, L, 256), bf16),
            jax.ShapeDtypeStruct((bsz, L, 256), bf16),
            jax.ShapeDtypeStruct((bsz, L, 256), bf16),
            jax.ShapeDtypeStruct((bsz, L, 512), f32),
            jax.ShapeDtypeStruct((bsz, L, IKW_PAD), f32),
        ],
        compiler_params=pltpu.CompilerParams(
            dimension_semantics=("parallel", "parallel"), vmem_limit_bytes=VMEM_LIMIT),
        name="premix",
    )(x, mod, mod, nw.reshape(1, d), w_in_b)


def _split3(x):
    hi = x.astype(bf16)
    r = x - hi.astype(f32)
    mid = r.astype(bf16)
    lo = (r - mid.astype(f32)).astype(bf16)
    return hi, mid, lo


def _hgrn2_chunk(qr, fr, v, gr, lb, onw, st_ref):
    C, S, W = HG_CHUNK, HG_SUB, HG_DK
    nsub = C // S
    heads = range(HG_HEADS)
    hs = lambda x, h: x[:, h * W:(h + 1) * W]

    q = _silu(qr) * (HG_DK ** -0.5)
    f = lb + (1.0 - lb) * jax.nn.sigmoid(fr)
    lf2 = jnp.log2(jnp.maximum(f, TINY))
    k = (1.0 - lb) * jax.nn.sigmoid(-fr)
    lk2 = jnp.log2(k)

    row = lax.broadcasted_iota(jnp.int32, (C, C), 0)
    col = lax.broadcasted_iota(jnp.int32, (C, C), 1)
    tril = jnp.where(col <= row, 1.0, 0.0).astype(bf16)
    g3 = _dot(tril, jnp.concatenate(_split3(lf2), axis=1))
    nw = HG_HEADS * W
    G2 = g3[:, :nw] + g3[:, nw:2 * nw] + g3[:, 2 * nw:]
    H = lk2 - G2
    g2_last = G2[C - 1:C, :]

    lane = lax.broadcasted_iota(jnp.int32, (S, C), 1)
    srow = lax.broadcasted_iota(jnp.int32, (S, C), 0)
    ones = jnp.ones((W, C), bf16)
    zpad = lambda x: jnp.concatenate([x, jnp.zeros((C - x.shape[0], W), x.dtype)], axis=0)

    o_inter, rsums, offs = [], [], []
    for h in heads:
        qh, gh, hh, lkh = hs(q, h), hs(G2, h), hs(H, h), hs(lk2, h)
        o_inter.append(_dot_nt((qh * jnp.exp2(gh)).astype(bf16), st_ref[h].astype(bf16)))
        parts = []
        for blk in range(nsub):
            r0 = blk * S
            qb, gb = qh[r0:r0 + S], gh[r0:r0 + S]
            for j in range(r0, r0 + S):
                parts.append(qb * jnp.exp2(jnp.minimum(gb + hh[j:j + 1, :], lkh[j:j + 1, :])))
        rsums.append(_dot(jnp.concatenate(parts, axis=0).astype(bf16), ones))
        off_h = []
        for blk in range(1, nsub):
            r0 = blk * S
            gref = gh[r0 - 1:r0, :]
            qd = qh[r0:r0 + S] * jnp.exp2(gh[r0:r0 + S] - gref)
            kd = jnp.exp2(jnp.minimum(gref + hh[:r0], lkh[:r0]))
            off_h.append(_dot_nt(qd.astype(bf16), zpad(kd).astype(bf16)))
        offs.append(off_h)

    outs = []
    for h in heads:
        vh = hs(v, h)
        a_rows = []
        for blk in range(nsub):
            r0 = blk * S
            a = jnp.zeros((S, C), f32)
            for j in range(S):
                a = jnp.where(lane == r0 + j, rsums[h][(r0 + j) * S:(r0 + j + 1) * S, :], a)
            a = jnp.where(lane - r0 <= srow, a, 0.0)
            if blk > 0:
                a = a + offs[h][blk - 1]
            a_rows.append(a)
        A = jnp.concatenate(a_rows, axis=0)
        o = o_inter[h] + _dot(A.astype(bf16), vh.astype(bf16))
        kg = jnp.exp2(hs(g2_last, h) + hs(H, h))
        st_ref[h] = st_ref[h] * jnp.exp2(hs(g2_last, h)) + _dot(vh.T.astype(bf16), kg.astype(bf16))
        o = o * lax.rsqrt(jnp.mean(o * o, axis=-1, keepdims=True) + EPS)
        outs.append(o * hs(onw, h) * _silu(hs(gr, h)))
    return jnp.concatenate(outs, axis=1)


def _hgrn2_kernel(q_ref, f_ref, v_ref, g_ref, lb_ref, onw_ref, o_ref, st_ref, *, n_chunks):
    @pl.when(pl.program_id(1) == 0)
    def _():
        st_ref[...] = jnp.zeros_like(st_ref)

    lb = lb_ref[...]
    onw = onw_ref[...]

    def body(c, carry):
        r0 = pl.multiple_of(c * HG_CHUNK, HG_CHUNK)
        sl = pl.ds(r0, HG_CHUNK)
        o = _hgrn2_chunk(q_ref[0, sl, :], f_ref[0, sl, :], v_ref[0, sl, :], g_ref[0, sl, :], lb, onw, st_ref)
        o_ref[0, sl, :] = o.astype(o_ref.dtype)
        return carry

    lax.fori_loop(0, n_chunks, body, 0)


def _hgrn2(hg, lb, onw, tl=512):
    bsz, L, _ = hg.shape
    part = lambda p: pl.BlockSpec((1, tl, HG_WIDTH), lambda b, i: (b, i, p))
    vec = pl.BlockSpec((1, HG_WIDTH), lambda b, i: (0, 0))
    return pl.pallas_call(
        functools.partial(_hgrn2_kernel, n_chunks=tl // HG_CHUNK),
        grid=(bsz, L // tl),
        in_specs=[part(0), part(1), part(2), part(3), vec, vec],
        out_specs=pl.BlockSpec((1, tl, HG_WIDTH), lambda b, i: (b, i, 0)),
        out_shape=jax.ShapeDtypeStruct((bsz, L, HG_WIDTH), bf16),
        scratch_shapes=[pltpu.VMEM((HG_HEADS, HG_DV, HG_DK), f32)],
        compiler_params=pltpu.CompilerParams(
            dimension_semantics=("parallel", "arbitrary"), vmem_limit_bytes=VMEM_LIMIT),
        name="hgrn2",
    )(hg, hg, hg, hg, lb.reshape(1, -1), onw.reshape(1, -1))


def _sortable(bits):
    return bits ^ ((bits >> 31) & jnp.int32(0x7FFFFFFF))


def _dsa_kernel(qT_ref, k_ref, vT_ref, iqT_ref, ik_ref, wT_ref, o_ref, key_ref, qh_ref,
                mx_ref, den_ref, oT_ref, lga_ref, lgb_ref, p_ref, *, topk, qb, tk):
    i = pl.program_id(1)
    t0 = i * qb
    nk = (t0 + qb) // tk
    sub = 64
    qpos = t0 + lax.broadcasted_iota(jnp.int32, (1, qb), 1)

    def score_tile(kt, carry, diagonal=False):
        for r in range(tk // sub):
            s0 = pl.multiple_of(kt * tk + r * sub, sub)
            ikt = ik_ref[0, pl.ds(s0, sub), :]
            acc = jnp.zeros((sub, qb), f32)
            for h in range(IDX_HEADS):
                sh = _dot(ikt, iqT_ref[0, h * IDX_DIM:(h + 1) * IDX_DIM, :])
                acc = acc + wT_ref[0, h:h + 1, :] * jnp.maximum(sh, 0.0)
            if diagonal:
                kpos = s0 + lax.broadcasted_iota(jnp.int32, (sub, qb), 0)
                acc = jnp.where(kpos <= qpos, acc, NEG_BIG)
            key_ref[pl.ds(s0, sub), :] = _sortable(pltpu.bitcast(acc + 0.0, jnp.int32))
        return carry

    lax.fori_loop(0, nk - 1, score_tile, 0)
    score_tile(nk - 1, 0, diagonal=True)

    def count_ge(cand):
        def tile(kt, acc):
            for r in range(tk // CNT_ROWS):
                s0 = pl.multiple_of(kt * tk + r * CNT_ROWS, CNT_ROWS)
                acc = acc + jnp.where(key_ref[pl.ds(s0, CNT_ROWS), :] >= cand, 1.0, 0.0)
            return acc
        acc = lax.fori_loop(0, nk, tile, jnp.zeros((CNT_ROWS, qb), f32))
        return jnp.sum(acc, axis=0, keepdims=True)

    def search_step(it, carry):
        v, c_ge, c_gt = carry
        trial = v + (jnp.int32(1) << (31 - it))
        cnt = count_ge(trial)
        ok = cnt >= topk
        return jnp.where(ok, trial, v), jnp.where(ok, cnt, c_ge), jnp.where(ok, c_gt, cnt)

    zeros_i = jnp.zeros((1, qb), jnp.int32)
    n_keys = (nk * tk).astype(f32)
    ustar, c_ge, c_gt = lax.fori_loop(
        0, 32, search_step,
        (jnp.full((1, qb), -2 ** 31, jnp.int32), jnp.zeros((1, qb), f32) + n_keys, jnp.zeros((1, qb), f32)))
    need = topk - c_gt
    cnt_eq = c_ge - c_gt
    take_all = qpos < topk

    def tie_limit(_):
        def step(it, p):
            trial = p | (jnp.int32(1) << (12 - it))

            def tile(kt, acc):
                s0 = pl.multiple_of(kt * tk, tk)
                kpos = s0 + lax.broadcasted_iota(jnp.int32, (tk, qb), 0)
                tied = jnp.where(kpos < trial, jnp.where(key_ref[pl.ds(s0, tk), :] == ustar, 1.0, 0.0), 0.0)
                return acc + jnp.sum(tied.reshape(tk // CNT_ROWS, CNT_ROWS, qb), axis=0)
            acc = lax.fori_loop(0, nk, tile, jnp.zeros((CNT_ROWS, qb), f32))
            return jnp.where(jnp.sum(acc, axis=0, keepdims=True) < need, trial, p)
        return lax.fori_loop(0, 13, step, zeros_i)

    has_excess = jnp.max(jnp.where(take_all, 0.0, cnt_eq - need)) > 0.0
    plim = lax.cond(has_excess, tie_limit, lambda _: jnp.full((1, qb), 2 ** 30, jnp.int32), 0)

    def bias_tile(kt, carry):
        s0 = pl.multiple_of(kt * tk, tk)
        u = key_ref[pl.ds(s0, tk), :]
        kpos = s0 + lax.broadcasted_iota(jnp.int32, (tk, qb), 0)
        tie = jnp.where(kpos <= plim, 0.0, NEG_BIG)
        bias = jnp.where(u > ustar, 0.0, jnp.where(u == ustar, tie, NEG_BIG))
        bias = jnp.where(take_all, 0.0, bias)
        key_ref[pl.ds(s0, tk), :] = pltpu.bitcast(jnp.where(kpos <= qpos, bias, NEG_BIG), jnp.int32)
        return carry

    lax.fori_loop(0, nk, bias_tile, 0)

    q_all = qT_ref[0] * jnp.asarray(SA_HEAD_DIM ** -0.5, bf16)
    hrow = lax.broadcasted_iota(jnp.int32, q_all.shape, 0) // SA_HEAD_DIM
    for h in range(SA_HEADS):
        qh_ref[:, h * qb:(h + 1) * qb] = jnp.where(hrow == h, q_all, jnp.zeros_like(q_all))
    mx_ref[...] = jnp.full(mx_ref.shape, NEG_BIG, f32)
    den_ref[...] = jnp.zeros_like(den_ref)
    oT_ref[...] = jnp.zeros_like(oT_ref)

    ta = tk // 2
    n_att = 2 * nk

    def logits(t, dst):
        s0 = pl.multiple_of(t * ta, ta)
        dst[...] = _dot(k_ref[0, pl.ds(s0, ta), :], qh_ref[...])

    def softmax_pv(t, src):
        s0 = pl.multiple_of(t * ta, ta)
        alphas = []
        for h in range(SA_HEADS):
            cols = slice(h * qb, (h + 1) * qb)
            mt = jnp.full((8, qb), NEG_BIG, f32)
            for r in range(0, ta, ATT_STRIP):
                lg = src[r:r + ATT_STRIP, cols] + pltpu.bitcast(key_ref[pl.ds(s0 + r, ATT_STRIP), :], f32)
                src[r:r + ATT_STRIP, cols] = lg
                mt = jnp.maximum(mt, jnp.max(lg.reshape(ATT_STRIP // 8, 8, qb), axis=0))
            m_old = mx_ref[h:h + 1, :]
            m_new = jnp.maximum(m_old, jnp.max(mt, axis=0, keepdims=True))
            alphas.append(jnp.exp(m_old - m_new))
            mx_ref[h:h + 1, :] = m_new
            dsum = jnp.zeros((8, qb), f32)
            for r in range(0, ta, ATT_STRIP):
                p = jnp.exp(src[r:r + ATT_STRIP, cols] - m_new)
                dsum = dsum + jnp.sum(p.reshape(ATT_STRIP // 8, 8, qb), axis=0)
                p_ref[h, r:r + ATT_STRIP, :] = p.astype(bf16)
            den_ref[h:h + 1, :] = alphas[h] * den_ref[h:h + 1, :] + jnp.sum(dsum, axis=0, keepdims=True)
        for h in range(SA_HEADS):
            hs = slice(h * SA_HEAD_DIM, (h + 1) * SA_HEAD_DIM)
            oT_ref[hs, :] = alphas[h] * oT_ref[hs, :] + _dot(vT_ref[0, hs, pl.ds(s0, ta)], p_ref[h])

    logits(0, lga_ref)

    def att_pair(j, carry):
        logits(2 * j + 1, lgb_ref)
        softmax_pv(2 * j, lga_ref)
        logits(jnp.minimum(2 * j + 2, n_att - 1), lga_ref)
        softmax_pv(2 * j + 1, lgb_ref)
        return carry

    lax.fori_loop(0, nk, att_pair, 0)
    for h in range(SA_HEADS):
        hs = slice(h * SA_HEAD_DIM, (h + 1) * SA_HEAD_DIM)
        oT_ref[hs, :] = oT_ref[hs, :] / den_ref[h:h + 1, :]
    o_ref[0] = oT_ref[...].T.astype(o_ref.dtype)


def _dsa(qT, k, vT, iqT, ik, wT, qb=512):
    bsz, L, _ = k.shape
    topk = min(IDX_TOPK_MAX, L // IDX_TOPK_DIV)
    assert L % qb == 0 and qb >= topk
    whole = lambda shape: pl.BlockSpec(shape, lambda b, i: (b, 0, 0), pipeline_mode=pl.Buffered(1))
    return pl.pallas_call(
        functools.partial(_dsa_kernel, topk=topk, qb=qb, tk=qb),
        grid=(bsz, L // qb),
        in_specs=[
            pl.BlockSpec((1, SA_WIDTH, qb), lambda b, i: (b, 0, i)),
            whole((1, L, SA_WIDTH)),
            whole((1, SA_WIDTH, L)),
            pl.BlockSpec((1, IDX_HEADS * IDX_DIM, qb), lambda b, i: (b, 0, i)),
            whole((1, L, IDX_DIM)),
            pl.BlockSpec((1, IDX_HEADS, qb), lambda b, i: (b, 0, i)),
        ],
        out_specs=pl.BlockSpec((1, qb, SA_WIDTH), lambda b, i: (b, i, 0)),
        out_shape=jax.ShapeDtypeStruct((bsz, L, SA_WIDTH), bf16),
        scratch_shapes=[pltpu.VMEM((L, qb), jnp.int32),
                        pltpu.VMEM((SA_WIDTH, SA_HEADS * qb), bf16),
                        pltpu.VMEM((8, qb), f32), pltpu.VMEM((8, qb), f32),
                        pltpu.VMEM((SA_WIDTH, qb), f32),
                        pltpu.VMEM((qb // 2, SA_HEADS * qb), f32), pltpu.VMEM((qb // 2, SA_HEADS * qb), f32),
                        pltpu.VMEM((SA_HEADS, qb // 2, qb), bf16)],
        compiler_params=pltpu.CompilerParams(
            dimension_semantics=("parallel", "arbitrary"), vmem_limit_bytes=VMEM_LIMIT),
        name="dsa",
    )(qT, k, vT, iqT, ik, wT)


def _conv_kernel(cu_ref, halo_ref, w_ref, b_ref, lnw_ref, lnb_ref, o_ref, h_ref, *, tl, strip):
    glu = lambda u: u[:, :CV_WIDTH] * jax.nn.sigmoid(u[:, CV_WIDTH:])
    halo = glu(halo_ref[0])
    h_ref[0, 0:CV_HALO, :] = jnp.where(pl.program_id(1) > 0, halo, 0.0)
    h_ref[0, CV_HALO:, :] = glu(cu_ref[0])
    for s in range(1, 8):
        h_ref[s, 0:tl + CV_HALO - 8, :] = h_ref[0, s:s + tl + CV_HALO - 8, :]
    lead = CV_HALO - (CV_FILTER - 1)
    for r in range(tl // strip):
        acc = jnp.zeros((strip, CV_WIDTH), f32) + b_ref[...]
        for j in range(CV_FILTER):
            s, base = (lead + j) % 8, (lead + j) // 8 * 8
            acc = acc + w_ref[j:j + 1, :] * h_ref[s, r * strip + base:r * strip + base + strip, :]
        mu = jnp.mean(acc, axis=-1, keepdims=True)
        cen = acc - mu
        var = jnp.mean(cen * cen, axis=-1, keepdims=True)
        y = cen * lax.rsqrt(var + EPS) * lnw_ref[...] + lnb_ref[...]
        o_ref[0, r * strip:(r + 1) * strip, :] = _silu(y).astype(o_ref.dtype)


def _conv(cu, cv_w, cv_b, ln_w, ln_b, tl=512, strip=64):
    bsz, L, _ = cu.shape
    hb = tl // CV_HALO
    vec = pl.BlockSpec((1, CV_WIDTH), lambda b, i: (0, 0))
    return pl.pallas_call(
        functools.partial(_conv_kernel, tl=tl, strip=strip),
        grid=(bsz, L // tl),
        in_specs=[
            pl.BlockSpec((1, tl, 2 * CV_WIDTH), lambda b, i: (b, i, 0)),
            pl.BlockSpec((1, CV_HALO, 2 * CV_WIDTH), lambda b, i: (b, jnp.maximum(i * hb - 1, 0), 0)),
            pl.BlockSpec((CV_FILTER, CV_WIDTH), lambda b, i: (0, 0)),
            vec, vec, vec,
        ],
        out_specs=pl.BlockSpec((1, tl, CV_WIDTH), lambda b, i: (b, i, 0)),
        out_shape=jax.ShapeDtypeStruct((bsz, L, CV_WIDTH), bf16),
        scratch_shapes=[pltpu.VMEM((8, tl + CV_HALO, CV_WIDTH), f32)],
        compiler_params=pltpu.CompilerParams(dimension_semantics=("parallel", "parallel")),
        name="conv",
    )(cu, cu, cv_w, cv_b.reshape(1, -1), ln_w.reshape(1, -1), ln_b.reshape(1, -1))


def _postmix_kernel(x_ref, oa_ref, ob_ref, oc_ref, g1_ref, sh_ref, sc_ref, g2_ref, nw_ref,
                    wo_ref, w1_ref, w2_ref, fw_ref, o_ref, acc_ref, *, ff_chunk, final):
    mix = (_dot(oa_ref[0], wo_ref[0:HG_WIDTH, :])
           + _dot(ob_ref[0], wo_ref[HG_WIDTH:HG_WIDTH + SA_WIDTH, :])
           + _dot(oc_ref[0], wo_ref[HG_WIDTH + SA_WIDTH:, :]))
    x1 = x_ref[0] + g1_ref[0] * mix
    h = _modulated_norm(x1, nw_ref[...], sc_ref[0], sh_ref[0]).astype(bf16)
    for c in range(D_FF // ff_chunk):
        u = jnp.maximum(_dot(h, w1_ref[:, c * ff_chunk:(c + 1) * ff_chunk]), 0.0)
        y = _dot((u * u).astype(bf16), w2_ref[c * ff_chunk:(c + 1) * ff_chunk, :])
        if c == 0:
            acc_ref[...] = y
        else:
            acc_ref[...] += y
    x2 = x1 + g2_ref[0] * acc_ref[...]
    if final:
        ms = jnp.mean(x2 * x2, axis=-1, keepdims=True)
        x2 = x2 * lax.rsqrt(ms + EPS) * fw_ref[...]
    o_ref[0] = x2


def _postmix(x, o_a, o_b, o_c, mod, nw, wo_b, w1_b, w2_b, fw, final, tm=512, ff_chunk=1024):
    bsz, L, d = x.shape
    tok = lambda w: pl.BlockSpec((1, tm, w), lambda b, i: (b, i, 0))
    modp = lambda j: pl.BlockSpec((1, 1, d), lambda b, i: (b, 0, j))
    const = lambda shape: pl.BlockSpec(shape, lambda b, i: (0, 0), pipeline_mode=pl.Buffered(1))
    return pl.pallas_call(
        functools.partial(_postmix_kernel, ff_chunk=ff_chunk, final=final),
        grid=(bsz, L // tm),
        in_specs=[
            tok(d), tok(HG_WIDTH), tok(SA_WIDTH), tok(CV_WIDTH),
            modp(2), modp(3), modp(4), modp(5),
            pl.BlockSpec((1, d), lambda b, i: (0, 0)),
            const((d, d)), const((d, D_FF)), const((D_FF, d)),
            pl.BlockSpec((1, d), lambda b, i: (0, 0)),
        ],
        out_specs=tok(d),
        out_shape=jax.ShapeDtypeStruct((bsz, L, d), f32),
        scratch_shapes=[pltpu.VMEM((tm, d), f32)],
        compiler_params=pltpu.CompilerParams(
            dimension_semantics=("parallel", "parallel"), vmem_limit_bytes=VMEM_LIMIT),
        name="postmix",
    )(x, o_a, o_b, o_c, mod, mod, mod, mod, nw.reshape(1, d), wo_b, w1_b, w2_b, fw.reshape(1, d))


def _pack_w_in(w_in):
    pad = jnp.zeros(w_in.shape[:-1] + (D_IN_PAD - w_in.shape[-1],), w_in.dtype)
    return jnp.concatenate([w_in[..., :3072], w_in[..., 3112:], w_in[..., 3072:3112], pad], axis=-1).astype(bf16)


def _lower_bounds(lb_logits):
    p = jax.nn.softmax(lb_logits.astype(f32), axis=0)
    return jnp.cumsum(p, axis=0) - p[0]


def kernel(x, c, ada_w, ada_b, norm_mix_w, norm_mlp_w, w_in, hg_lb_logits, hg_onorm_w, cv_w, cv_b,
           cv_ln_w, cv_ln_b, w_out, mlp_w1, mlp_w2, final_norm_w):
    bsz, L, d = x.shape
    depth = w_in.shape[0]
    mods = _ada(c, ada_w, ada_b)
    lbs = _lower_bounds(hg_lb_logits)
    w_in_b = _pack_w_in(w_in)
    wo_b, w1_b, w2_b = w_out.astype(bf16), mlp_w1.astype(bf16), mlp_w2.astype(bf16)
    tr = lambda a: jnp.swapaxes(a, 1, 2)
    for l in range(depth):
        mod = mods[l].reshape(bsz, 1, N_MOD * d)
        hg, sq, sk, sv, iq, cu, ikw = _premix(x, mod, norm_mix_w[l], w_in_b[l])
        o_a = _hgrn2(hg, lbs[l], hg_onorm_w[l])
        o_b = _dsa(tr(sq), sk, tr(sv), tr(iq), ikw[..., :IDX_DIM].astype(bf16),
                   tr(ikw[..., IDX_DIM:IDX_DIM + IDX_HEADS]))
        o_c = _conv(cu, cv_w[l], cv_b[l], cv_ln_w[l], cv_ln_b[l])
        x = _postmix(x, o_a, o_b, o_c, mod, norm_mlp_w[l], wo_b[l], w1_b[l], w2_b[l],
                     final_norm_w, final=(l == depth - 1))
    return x
```

```python
import functools

import jax
import jax.numpy as jnp
from jax import lax
from jax.experimental import pallas as pl
from jax.experimental.pallas import tpu as pltpu

f32 = jnp.float32
bf16 = jnp.bfloat16

D_MODEL = 1024
DEPTH = 4
HG_HEADS = 4
HG_DK = 128
HG_DV = 128
HG_WIDTH = HG_HEADS * HG_DV
HG_CHUNK = 64
HG_SUB = 8
SA_HEADS = 4
SA_HEAD_DIM = 64
SA_WIDTH = SA_HEADS * SA_HEAD_DIM
IDX_HEADS = 8
IDX_DIM = 32
IDX_TOPK_MAX = 256
IDX_TOPK_DIV = 4
CV_WIDTH = 256
CV_FILTER = 31
CV_HALO = 32
CNT_ROWS = 32
ATT_STRIP = 64
D_FF = 4 * D_MODEL
N_MOD = 6
EPS = 1e-6
NEG_BIG = -1e30
TINY = 1e-30

IN_SIZES = (512, 512, 512, 512, 256, 256, 256, 256, 32, 8, 512)
IKW_PAD = 128
D_IN_PAD = 2048 + 768 + 256 + 512 + IKW_PAD

VMEM_LIMIT = 56 * 1024 * 1024


def _dot(a, b):
    return jnp.dot(a, b, preferred_element_type=f32)


def _dot_nt(a, b):
    return lax.dot_general(a, b, (((1,), (1,)), ((), ())), preferred_element_type=f32)


def _silu(x):
    return x * jax.nn.sigmoid(x)


def _ada_kernel(c_ref, w_ref, b_ref, o_ref):
    ca = _silu(c_ref[...])
    o_ref[0] = _dot(ca.astype(bf16), w_ref[0].astype(bf16)) + b_ref[0]


def _ada(c, ada_w, ada_b):
    depth, d, _ = ada_w.shape
    bsz = c.shape[0]
    return pl.pallas_call(
        _ada_kernel,
        grid=(depth, N_MOD),
        in_specs=[
            pl.BlockSpec((bsz, d), lambda l, j: (0, 0)),
            pl.BlockSpec((1, d, d), lambda l, j: (l, 0, j)),
            pl.BlockSpec((1, 1, d), lambda l, j: (l, 0, j)),
        ],
        out_specs=pl.BlockSpec((1, bsz, d), lambda l, j: (l, 0, j)),
        out_shape=jax.ShapeDtypeStruct((depth, bsz, N_MOD * d), f32),
        name="ada",
    )(c, ada_w, ada_b.reshape(depth, 1, N_MOD * d))


def _modulated_norm(x, nw, sc, sh):
    ms = jnp.mean(x * x, axis=-1, keepdims=True)
    y = x * lax.rsqrt(ms + EPS) * nw
    return y * (1.0 + sc) + sh


def _premix_kernel(x_ref, sh_ref, sc_ref, nw_ref, w_ref,
                   hg_ref, sq_ref, sk_ref, sv_ref, iq_ref, cu_ref, ikw_ref):
    h = _modulated_norm(x_ref[0], nw_ref[...], sc_ref[0], sh_ref[0]).astype(bf16)
    hg_ref[0] = _dot(h, w_ref[:, 0:2048])
    sq_ref[0] = _dot(h, w_ref[:, 2048:2304]).astype(bf16)
    sk_ref[0] = _dot(h, w_ref[:, 2304:2560]).astype(bf16)
    sv_ref[0] = _dot(h, w_ref[:, 2560:2816]).astype(bf16)
    iq_ref[0] = _dot(h, w_ref[:, 2816:3072]).astype(bf16)
    cu_ref[0] = _dot(h, w_ref[:, 3072:3584])
    ikw_ref[0] = _dot(h, w_ref[:, 3584:3712])


def _premix(x, mod, nw, w_in_b, tm=512):
    bsz, L, d = x.shape
    tok = lambda w: pl.BlockSpec((1, tm, w), lambda b, i: (b, i, 0))
    return pl.pallas_call(
        _premix_kernel,
        grid=(bsz, L // tm),
        in_specs=[
            tok(d),
            pl.BlockSpec((1, 1, d), lambda b, i: (b, 0, 0)),
            pl.BlockSpec((1, 1, d), lambda b, i: (b, 0, 1)),
            pl.BlockSpec((1, d), lambda b, i: (0, 0)),
            pl.BlockSpec((d, D_IN_PAD), lambda b, i: (0, 0), pipeline_mode=pl.Buffered(1)),
        ],
        out_specs=[tok(2048), tok(256), tok(256), tok(256), tok(256), tok(512), tok(IKW_PAD)],
        out_shape=[
            jax.ShapeDtypeStruct((bsz, L, 2048), f32),
            jax.ShapeDtypeStruct((bsz, L, 256), bf16),
            jax.ShapeDtypeStruct((bsz, L, 256), bf16),
            jax.ShapeDtypeStruct((bsz, L, 256), bf16),
            jax.ShapeDtypeStruct((bsz, L, 256), bf16),
            jax.ShapeDtypeStruct((bsz, L, 512), f32),
            jax.ShapeDtypeStruct((bsz, L, IKW_PAD), f32),
        ],
        compiler_params=pltpu.CompilerParams(
            dimension_semantics=("parallel", "parallel"), vmem_limit_bytes=VMEM_LIMIT),
        name="premix",
    )(x, mod, mod, nw.reshape(1, d), w_in_b)


def _split3(x):
    hi = x.astype(bf16)
    r = x - hi.astype(f32)
    mid = r.astype(bf16)
    lo = (r - mid.astype(f32)).astype(bf16)
    return hi, mid, lo


def _hgrn2_chunk(qr, fr, v, gr, lb, onw, st_ref):
    C, S, W = HG_CHUNK, HG_SUB, HG_DK
    nsub = C // S
    heads = range(HG_HEADS)
    hs = lambda x, h: x[:, h * W:(h + 1) * W]

    q = _silu(qr) * (HG_DK ** -0.5)
    f = lb + (1.0 - lb) * jax.nn.sigmoid(fr)
    lf2 = jnp.log2(jnp.maximum(f, TINY))
    k = (1.0 - lb) * jax.nn.sigmoid(-fr)
    lk2 = jnp.log2(k)

    row = lax.broadcasted_iota(jnp.int32, (C, C), 0)
    col = lax.broadcasted_iota(jnp.int32, (C, C), 1)
    tril = jnp.where(col <= row, 1.0, 0.0).astype(bf16)
    g3 = _dot(tril, jnp.concatenate(_split3(lf2), axis=1))
    nw = HG_HEADS * W
    G2 = g3[:, :nw] + g3[:, nw:2 * nw] + g3[:, 2 * nw:]
    H = lk2 - G2
    g2_last = G2[C - 1:C, :]

    lane = lax.broadcasted_iota(jnp.int32, (S, C), 1)
    srow = lax.broadcasted_iota(jnp.int32, (S, C), 0)
    ones = jnp.ones((W, C), bf16)
    zpad = lambda x: jnp.concatenate([x, jnp.zeros((C - x.shape[0], W), x.dtype)], axis=0)

    o_inter, rsums, offs = [], [], []
    for h in heads:
        qh, gh, hh, lkh = hs(q, h), hs(G2, h), hs(H, h), hs(lk2, h)
        o_inter.append(_dot_nt((qh * jnp.exp2(gh)).astype(bf16), st_ref[h].astype(bf16)))
        parts = []
        for blk in range(nsub):
            r0 = blk * S
            qb, gb = qh[r0:r0 + S], gh[r0:r0 + S]
            for j in range(r0, r0 + S):
                parts.append(qb * jnp.exp2(jnp.minimum(gb + hh[j:j + 1, :], lkh[j:j + 1, :])))
        rsums.append(_dot(jnp.concatenate(parts, axis=0).astype(bf16), ones))
        off_h = []
        for blk in range(1, nsub):
            r0 = blk * S
            gref = gh[r0 - 1:r0, :]
            qd = qh[r0:r0 + S] * jnp.exp2(gh[r0:r0 + S] - gref)
            kd = jnp.exp2(jnp.minimum(gref + hh[:r0], lkh[:r0]))
            off_h.append(_dot_nt(qd.astype(bf16), zpad(kd).astype(bf16)))
        offs.append(off_h)

    outs = []
    for h in heads:
        vh = hs(v, h)
        a_rows = []
        for blk in range(nsub):
            r0 = blk * S
            a = jnp.zeros((S, C), f32)
            for j in range(S):
                a = jnp.where(lane == r0 + j, rsums[h][(r0 + j) * S:(r0 + j + 1) * S, :], a)
            a = jnp.where(lane - r0 <= srow, a, 0.0)
            if blk > 0:
                a = a + offs[h][blk - 1]
            a_rows.append(a)
        A = jnp.concatenate(a_rows, axis=0)
        o = o_inter[h] + _dot(A.astype(bf16), vh.astype(bf16))
        kg = jnp.exp2(hs(g2_last, h) + hs(H, h))
        st_ref[h] = st_ref[h] * jnp.exp2(hs(g2_last, h)) + _dot(vh.T.astype(bf16), kg.astype(bf16))
        o = o * lax.rsqrt(jnp.mean(o * o, axis=-1, keepdims=True) + EPS)
        outs.append(o * hs(onw, h) * _silu(hs(gr, h)))
    return jnp.concatenate(outs, axis=1)


def _hgrn2_kernel(q_ref, f_ref, v_ref, g_ref, lb_ref, onw_ref, o_ref, st_ref, *, n_chunks):
    @pl.when(pl.program_id(1) == 0)
    def _():
        st_ref[...] = jnp.zeros_like(st_ref)

    lb = lb_ref[...]
    onw = onw_ref[...]

    def body(c, carry):
        r0 = pl.multiple_of(c * HG_CHUNK, HG_CHUNK)
        sl = pl.ds(r0, HG_CHUNK)
        o = _hgrn2_chunk(q_ref[0, sl, :], f_ref[0, sl, :], v_ref[0, sl, :], g_ref[0, sl, :], lb, onw, st_ref)
        o_ref[0, sl, :] = o.astype(o_ref.dtype)
        return carry

    lax.fori_loop(0, n_chunks, body, 0)


def _hgrn2(hg, lb, onw, tl=512):
    bsz, L, _ = hg.shape
    part = lambda p: pl.BlockSpec((1, tl, HG_WIDTH), lambda b, i: (b, i, p))
    vec = pl.BlockSpec((1, HG_WIDTH), lambda b, i: (0, 0))
    return pl.pallas_call(
        functools.partial(_hgrn2_kernel, n_chunks=tl // HG_CHUNK),
        grid=(bsz, L // tl),
        in_specs=[part(0), part(1), part(2), part(3), vec, vec],
        out_specs=pl.BlockSpec((1, tl, HG_WIDTH), lambda b, i: (b, i, 0)),
        out_shape=jax.ShapeDtypeStruct((bsz, L, HG_WIDTH), bf16),
        scratch_shapes=[pltpu.VMEM((HG_HEADS, HG_DV, HG_DK), f32)],
        compiler_params=pltpu.CompilerParams(
            dimension_semantics=("parallel", "arbitrary"), vmem_limit_bytes=VMEM_LIMIT),
        name="hgrn2",
    )(hg, hg, hg, hg, lb.reshape(1, -1), onw.reshape(1, -1))


def _sortable(bits):
    return bits ^ ((bits >> 31) & jnp.int32(0x7FFFFFFF))


def _dsa_kernel(qT_ref, k_ref, vT_ref, iqT_ref, ik_ref, wT_ref, o_ref, key_ref, qh_ref,
                mx_ref, den_ref, oT_ref, lga_ref, lgb_ref, p_ref, *, topk, qb, tk, n_qblocks):
    i = pl.program_id(1)
    t0 = i * qb
    nk = (t0 + qb) // tk
    sub = 64
    qpos = t0 + lax.broadcasted_iota(jnp.int32, (1, qb), 1)

    def score_tile(kt, carry, diagonal=False):
        for r in range(tk // sub):
            s0 = pl.multiple_of(kt * tk + r * sub, sub)
            ikt = ik_ref[0, pl.ds(s0, sub), :]
            acc = jnp.zeros((sub, qb), f32)
            for h in range(IDX_HEADS):
                sh = _dot(ikt, iqT_ref[0, h * IDX_DIM:(h + 1) * IDX_DIM, :])
                acc = acc + wT_ref[0, h:h + 1, :] * jnp.maximum(sh, 0.0)
            if diagonal:
                kpos = s0 + lax.broadcasted_iota(jnp.int32, (sub, qb), 0)
                acc = jnp.where(kpos <= qpos, acc, NEG_BIG)
            key_ref[pl.ds(s0, sub), :] = _sortable(pltpu.bitcast(acc + 0.0, jnp.int32))
        return carry

    lax.fori_loop(0, nk - 1, score_tile, 0)
    score_tile(nk - 1, 0, diagonal=True)

    def search(n_tiles):
        def count_ge(cand):
            acc = jnp.zeros((CNT_ROWS, qb), f32)
            for r in range(0, n_tiles * tk, CNT_ROWS):
                acc = acc + jnp.where(key_ref[r:r + CNT_ROWS, :] >= cand, 1.0, 0.0)
            return jnp.sum(acc, axis=0, keepdims=True)

        def step(it, carry):
            v, c_ge, c_gt = carry
            trial = v + (jnp.int32(1) << (31 - it))
            cnt = count_ge(trial)
            ok = cnt >= topk
            return jnp.where(ok, trial, v), jnp.where(ok, cnt, c_ge), jnp.where(ok, c_gt, cnt)

        return lax.fori_loop(
            0, 32, step,
            (jnp.full((1, qb), -2 ** 31, jnp.int32), jnp.full((1, qb), float(n_tiles * tk), f32),
             jnp.zeros((1, qb), f32)))

    zeros_i = jnp.zeros((1, qb), jnp.int32)
    ustar, c_ge, c_gt = lax.switch(i, [functools.partial(search, n + 1) for n in range(n_qblocks)])
    need = topk - c_gt
    cnt_eq = c_ge - c_gt
    take_all = qpos < topk

    def tie_limit(_):
        def step(it, p):
            trial = p | (jnp.int32(1) << (12 - it))

            def tile(kt, acc):
                s0 = pl.multiple_of(kt * tk, tk)
                kpos = s0 + lax.broadcasted_iota(jnp.int32, (tk, qb), 0)
                tied = jnp.where(kpos < trial, jnp.where(key_ref[pl.ds(s0, tk), :] == ustar, 1.0, 0.0), 0.0)
                return acc + jnp.sum(tied.reshape(tk // CNT_ROWS, CNT_ROWS, qb), axis=0)
            acc = lax.fori_loop(0, nk, tile, jnp.zeros((CNT_ROWS, qb), f32))
            return jnp.where(jnp.sum(acc, axis=0, keepdims=True) < need, trial, p)
        return lax.fori_loop(0, 13, step, zeros_i)

    has_excess = jnp.max(jnp.where(take_all, 0.0, cnt_eq - need)) > 0.0
    plim = lax.cond(has_excess, tie_limit, lambda _: jnp.full((1, qb), 2 ** 30, jnp.int32), 0)

    def bias_tile(kt, carry):
        s0 = pl.multiple_of(kt * tk, tk)
        u = key_ref[pl.ds(s0, tk), :]
        kpos = s0 + lax.broadcasted_iota(jnp.int32, (tk, qb), 0)
        tie = jnp.where(kpos <= plim, 0.0, NEG_BIG)
        bias = jnp.where(u > ustar, 0.0, jnp.where(u == ustar, tie, NEG_BIG))
        bias = jnp.where(take_all, 0.0, bias)
        key_ref[pl.ds(s0, tk), :] = pltpu.bitcast(jnp.where(kpos <= qpos, bias, NEG_BIG), jnp.int32)
        return carry

    lax.fori_loop(0, nk, bias_tile, 0)

    q_all = qT_ref[0] * jnp.asarray(SA_HEAD_DIM ** -0.5, bf16)
    hrow = lax.broadcasted_iota(jnp.int32, q_all.shape, 0) // SA_HEAD_DIM
    for h in range(SA_HEADS):
        qh_ref[:, h * qb:(h + 1) * qb] = jnp.where(hrow == h, q_all, jnp.zeros_like(q_all))
    mx_ref[...] = jnp.full(mx_ref.shape, NEG_BIG, f32)
    den_ref[...] = jnp.zeros_like(den_ref)
    oT_ref[...] = jnp.zeros_like(oT_ref)

    ta = tk // 2
    n_att = 2 * nk

    def logits(t, dst):
        s0 = pl.multiple_of(t * ta, ta)
        dst[...] = _dot(k_ref[0, pl.ds(s0, ta), :], qh_ref[...])

    def softmax_pv(t, src):
        s0 = pl.multiple_of(t * ta, ta)
        alphas = []
        for h in range(SA_HEADS):
            cols = slice(h * qb, (h + 1) * qb)
            mt = jnp.full((8, qb), NEG_BIG, f32)
            for r in range(0, ta, ATT_STRIP):
                lg = src[r:r + ATT_STRIP, cols] + pltpu.bitcast(key_ref[pl.ds(s0 + r, ATT_STRIP), :], f32)
                src[r:r + ATT_STRIP, cols] = lg
                mt = jnp.maximum(mt, jnp.max(lg.reshape(ATT_STRIP // 8, 8, qb), axis=0))
            m_old = mx_ref[h:h + 1, :]
            m_new = jnp.maximum(m_old, jnp.max(mt, axis=0, keepdims=True))
            alphas.append(jnp.exp(m_old - m_new))
            mx_ref[h:h + 1, :] = m_new
            dsum = jnp.zeros((8, qb), f32)
            for r in range(0, ta, ATT_STRIP):
                p = jnp.exp(src[r:r + ATT_STRIP, cols] - m_new)
                dsum = dsum + jnp.sum(p.reshape(ATT_STRIP // 8, 8, qb), axis=0)
                p_ref[h, r:r + ATT_STRIP, :] = p.astype(bf16)
            den_ref[h:h + 1, :] = alphas[h] * den_ref[h:h + 1, :] + jnp.sum(dsum, axis=0, keepdims=True)
        for h in range(SA_HEADS):
            hs = slice(h * SA_HEAD_DIM, (h + 1) * SA_HEAD_DIM)
            oT_ref[hs, :] = alphas[h] * oT_ref[hs, :] + _dot(vT_ref[0, hs, pl.ds(s0, ta)], p_ref[h])

    logits(0, lga_ref)

    def att_pair(j, carry):
        logits(2 * j + 1, lgb_ref)
        softmax_pv(2 * j, lga_ref)
        logits(jnp.minimum(2 * j + 2, n_att - 1), lga_ref)
        softmax_pv(2 * j + 1, lgb_ref)
        return carry

    lax.fori_loop(0, nk, att_pair, 0)
    for h in range(SA_HEADS):
        hs = slice(h * SA_HEAD_DIM, (h + 1) * SA_HEAD_DIM)
        oT_ref[hs, :] = oT_ref[hs, :] / den_ref[h:h + 1, :]
    o_ref[0] = oT_ref[...].T.astype(o_ref.dtype)


def _dsa(qT, k, vT, iqT, ik, wT, qb=512):
    bsz, L, _ = k.shape
    topk = min(IDX_TOPK_MAX, L // IDX_TOPK_DIV)
    assert L % qb == 0 and qb >= topk
    whole = lambda shape: pl.BlockSpec(shape, lambda b, i: (b, 0, 0), pipeline_mode=pl.Buffered(1))
    return pl.pallas_call(
        functools.partial(_dsa_kernel, topk=topk, qb=qb, tk=qb, n_qblocks=L // qb),
        grid=(bsz, L // qb),
        in_specs=[
            pl.BlockSpec((1, SA_WIDTH, qb), lambda b, i: (b, 0, i)),
            whole((1, L, SA_WIDTH)),
            whole((1, SA_WIDTH, L)),
            pl.BlockSpec((1, IDX_HEADS * IDX_DIM, qb), lambda b, i: (b, 0, i)),
            whole((1, L, IDX_DIM)),
            pl.BlockSpec((1, IDX_HEADS, qb), lambda b, i: (b, 0, i)),
        ],
        out_specs=pl.BlockSpec((1, qb, SA_WIDTH), lambda b, i: (b, i, 0)),
        out_shape=jax.ShapeDtypeStruct((bsz, L, SA_WIDTH), bf16),
        scratch_shapes=[pltpu.VMEM((L, qb), jnp.int32),
                        pltpu.VMEM((SA_WIDTH, SA_HEADS * qb), bf16),
                        pltpu.VMEM((8, qb), f32), pltpu.VMEM((8, qb), f32),
                        pltpu.VMEM((SA_WIDTH, qb), f32),
                        pltpu.VMEM((qb // 2, SA_HEADS * qb), f32), pltpu.VMEM((qb // 2, SA_HEADS * qb), f32),
                        pltpu.VMEM((SA_HEADS, qb // 2, qb), bf16)],
        compiler_params=pltpu.CompilerParams(
            dimension_semantics=("parallel", "arbitrary"), vmem_limit_bytes=VMEM_LIMIT),
        name="dsa",
    )(qT, k, vT, iqT, ik, wT)


def _conv_kernel(cu_ref, halo_ref, w_ref, b_ref, lnw_ref, lnb_ref, o_ref, h_ref, *, tl, strip):
    glu = lambda u: u[:, :CV_WIDTH] * jax.nn.sigmoid(u[:, CV_WIDTH:])
    halo = glu(halo_ref[0])
    h_ref[0, 0:CV_HALO, :] = jnp.where(pl.program_id(1) > 0, halo, 0.0)
    h_ref[0, CV_HALO:, :] = glu(cu_ref[0])
    for s in range(1, 8):
        h_ref[s, 0:tl + CV_HALO - 8, :] = h_ref[0, s:s + tl + CV_HALO - 8, :]
    lead = CV_HALO - (CV_FILTER - 1)
    for r in range(tl // strip):
        acc = jnp.zeros((strip, CV_WIDTH), f32) + b_ref[...]
        for j in range(CV_FILTER):
            s, base = (lead + j) % 8, (lead + j) // 8 * 8
            acc = acc + w_ref[j:j + 1, :] * h_ref[s, r * strip + base:r * strip + base + strip, :]
        mu = jnp.mean(acc, axis=-1, keepdims=True)
        cen = acc - mu
        var = jnp.mean(cen * cen, axis=-1, keepdims=True)
        y = cen * lax.rsqrt(var + EPS) * lnw_ref[...] + lnb_ref[...]
        o_ref[0, r * strip:(r + 1) * strip, :] = _silu(y).astype(o_ref.dtype)


def _conv(cu, cv_w, cv_b, ln_w, ln_b, tl=512, strip=64):
    bsz, L, _ = cu.shape
    hb = tl // CV_HALO
    vec = pl.BlockSpec((1, CV_WIDTH), lambda b, i: (0, 0))
    return pl.pallas_call(
        functools.partial(_conv_kernel, tl=tl, strip=strip),
        grid=(bsz, L // tl),
        in_specs=[
            pl.BlockSpec((1, tl, 2 * CV_WIDTH), lambda b, i: (b, i, 0)),
            pl.BlockSpec((1, CV_HALO, 2 * CV_WIDTH), lambda b, i: (b, jnp.maximum(i * hb - 1, 0), 0)),
            pl.BlockSpec((CV_FILTER, CV_WIDTH), lambda b, i: (0, 0)),
            vec, vec, vec,
        ],
        out_specs=pl.BlockSpec((1, tl, CV_WIDTH), lambda b, i: (b, i, 0)),
        out_shape=jax.ShapeDtypeStruct((bsz, L, CV_WIDTH), bf16),
        scratch_shapes=[pltpu.VMEM((8, tl + CV_HALO, CV_WIDTH), f32)],
        compiler_params=pltpu.CompilerParams(dimension_semantics=("parallel", "parallel")),
        name="conv",
    )(cu, cu, cv_w, cv_b.reshape(1, -1), ln_w.reshape(1, -1), ln_b.reshape(1, -1))


def _postmix_kernel(x_ref, oa_ref, ob_ref, oc_ref, g1_ref, sh_ref, sc_ref, g2_ref, nw_ref,
                    wo_ref, w1_ref, w2_ref, fw_ref, o_ref, acc_ref, *, ff_chunk, final):
    mix = (_dot(oa_ref[0], wo_ref[0:HG_WIDTH, :])
           + _dot(ob_ref[0], wo_ref[HG_WIDTH:HG_WIDTH + SA_WIDTH, :])
           + _dot(oc_ref[0], wo_ref[HG_WIDTH + SA_WIDTH:, :]))
    x1 = x_ref[0] + g1_ref[0] * mix
    h = _modulated_norm(x1, nw_ref[...], sc_ref[0], sh_ref[0]).astype(bf16)
    for c in range(D_FF // ff_chunk):
        u = jnp.maximum(_dot(h, w1_ref[:, c * ff_chunk:(c + 1) * ff_chunk]), 0.0)
        y = _dot((u * u).astype(bf16), w2_ref[c * ff_chunk:(c + 1) * ff_chunk, :])
        if c == 0:
            acc_ref[...] = y
        else:
            acc_ref[...] += y
    x2 = x1 + g2_ref[0] * acc_ref[...]
    if final:
        ms = jnp.mean(x2 * x2, axis=-1, keepdims=True)
        x2 = x2 * lax.rsqrt(ms + EPS) * fw_ref[...]
    o_ref[0] = x2


def _postmix(x, o_a, o_b, o_c, mod, nw, wo_b, w1_b, w2_b, fw, final, tm=512, ff_chunk=1024):
    bsz, L, d = x.shape
    tok = lambda w: pl.BlockSpec((1, tm, w), lambda b, i: (b, i, 0))
    modp = lambda j: pl.BlockSpec((1, 1, d), lambda b, i: (b, 0, j))
    const = lambda shape: pl.BlockSpec(shape, lambda b, i: (0, 0), pipeline_mode=pl.Buffered(1))
    return pl.pallas_call(
        functools.partial(_postmix_kernel, ff_chunk=ff_chunk, final=final),
        grid=(bsz, L // tm),
        in_specs=[
            tok(d), tok(HG_WIDTH), tok(SA_WIDTH), tok(CV_WIDTH),
            modp(2), modp(3), modp(4), modp(5),
            pl.BlockSpec((1, d), lambda b, i: (0, 0)),
            const((d, d)), const((d, D_FF)), const((D_FF, d)),
            pl.BlockSpec((1, d), lambda b, i: (0, 0)),
        ],
        out_specs=tok(d),
        out_shape=jax.ShapeDtypeStruct((bsz, L, d), f32),
        scratch_shapes=[pltpu.VMEM((tm, d), f32)],
        compiler_params=pltpu.CompilerParams(
            dimension_semantics=("parallel", "parallel"), vmem_limit_bytes=VMEM_LIMIT),
        name="postmix",
    )(x, o_a, o_b, o_c, mod, mod, mod, mod, nw.reshape(1, d), wo_b, w1_b, w2_b, fw.reshape(1, d))


def _pack_w_in(w_in):
    pad = jnp.zeros(w_in.shape[:-1] + (D_IN_PAD - w_in.shape[-1],), w_in.dtype)
    return jnp.concatenate([w_in[..., :3072], w_in[..., 3112:], w_in[..., 3072:3112], pad], axis=-1).astype(bf16)


def _lower_bounds(lb_logits):
    p = jax.nn.softmax(lb_logits.astype(f32), axis=0)
    return jnp.cumsum(p, axis=0) - p[0]


def kernel(x, c, ada_w, ada_b, norm_mix_w, norm_mlp_w, w_in, hg_lb_logits, hg_onorm_w, cv_w, cv_b,
           cv_ln_w, cv_ln_b, w_out, mlp_w1, mlp_w2, final_norm_w):
    bsz, L, d = x.shape
    depth = w_in.shape[0]
    mods = _ada(c, ada_w, ada_b)
    lbs = _lower_bounds(hg_lb_logits)
    w_in_b = _pack_w_in(w_in)
    wo_b, w1_b, w2_b = w_out.astype(bf16), mlp_w1.astype(bf16), mlp_w2.astype(bf16)
    tr = lambda a: jnp.swapaxes(a, 1, 2)
    for l in range(depth):
        mod = mods[l].reshape(bsz, 1, N_MOD * d)
        hg, sq, sk, sv, iq, cu, ikw = _premix(x, mod, norm_mix_w[l], w_in_b[l])
        o_a = _hgrn2(hg, lbs[l], hg_onorm_w[l])
        o_b = _dsa(tr(sq), sk, tr(sv), tr(iq), ikw[..., :IDX_DIM].astype(bf16),
                   tr(ikw[..., IDX_DIM:IDX_DIM + IDX_HEADS]))
        o_c = _conv(cu, cv_w[l], cv_b[l], cv_ln_w[l], cv_ln_b[l])
        x = _postmix(x, o_a, o_b, o_c, mod, norm_mlp_w[l], wo_b[l], w1_b[l], w2_b[l],
                     final_norm_w, final=(l == depth - 1))
    return x
```

```python
import functools

import jax
import jax.numpy as jnp
from jax import lax
from jax.experimental import pallas as pl
from jax.experimental.pallas import tpu as pltpu

f32 = jnp.float32
bf16 = jnp.bfloat16

D_MODEL = 1024
DEPTH = 4
HG_HEADS = 4
HG_DK = 128
HG_DV = 128
HG_WIDTH = HG_HEADS * HG_DV
HG_CHUNK = 64
HG_SUB = 8
SA_HEADS = 4
SA_HEAD_DIM = 64
SA_WIDTH = SA_HEADS * SA_HEAD_DIM
IDX_HEADS = 8
IDX_DIM = 32
IDX_TOPK_MAX = 256
IDX_TOPK_DIV = 4
CV_WIDTH = 256
CV_FILTER = 31
CV_HALO = 32
CNT_ROWS = 32
ATT_STRIP = 64
D_FF = 4 * D_MODEL
N_MOD = 6
EPS = 1e-6
NEG_BIG = -1e30
TINY = 1e-30

IN_SIZES = (512, 512, 512, 512, 256, 256, 256, 256, 32, 8, 512)
IKW_PAD = 128
D_IN_PAD = 2048 + 768 + 256 + 512 + IKW_PAD

VMEM_LIMIT = 56 * 1024 * 1024


def _dot(a, b):
    return jnp.dot(a, b, preferred_element_type=f32)


def _dot_nt(a, b):
    return lax.dot_general(a, b, (((1,), (1,)), ((), ())), preferred_element_type=f32)


def _silu(x):
    return x * jax.nn.sigmoid(x)


def _ada_kernel(c_ref, w_ref, b_ref, o_ref):
    ca = _silu(c_ref[...])
    o_ref[0] = _dot(ca.astype(bf16), w_ref[0].astype(bf16)) + b_ref[0]


def _ada(c, ada_w, ada_b):
    depth, d, _ = ada_w.shape
    bsz = c.shape[0]
    return pl.pallas_call(
        _ada_kernel,
        grid=(depth, N_MOD),
        in_specs=[
            pl.BlockSpec((bsz, d), lambda l, j: (0, 0)),
            pl.BlockSpec((1, d, d), lambda l, j: (l, 0, j)),
            pl.BlockSpec((1, 1, d), lambda l, j: (l, 0, j)),
        ],
        out_specs=pl.BlockSpec((1, bsz, d), lambda l, j: (l, 0, j)),
        out_shape=jax.ShapeDtypeStruct((depth, bsz, N_MOD * d), f32),
        name="ada",
    )(c, ada_w, ada_b.reshape(depth, 1, N_MOD * d))


def _modulated_norm(x, nw, sc, sh):
    ms = jnp.mean(x * x, axis=-1, keepdims=True)
    y = x * lax.rsqrt(ms + EPS) * nw
    return y * (1.0 + sc) + sh


def _premix_kernel(x_ref, sh_ref, sc_ref, nw_ref, w_ref,
                   hg_ref, sq_ref, sk_ref, sv_ref, iq_ref, cu_ref, ikw_ref):
    h = _modulated_norm(x_ref[0], nw_ref[...], sc_ref[0], sh_ref[0]).astype(bf16)
    hg_ref[0] = _dot(h, w_ref[:, 0:2048])
    sq_ref[0] = _dot(h, w_ref[:, 2048:2304]).astype(bf16)
    sk_ref[0] = _dot(h, w_ref[:, 2304:2560]).astype(bf16)
    sv_ref[0] = _dot(h, w_ref[:, 2560:2816]).astype(bf16)
    iq_ref[0] = _dot(h, w_ref[:, 2816:3072]).astype(bf16)
    cu_ref[0] = _dot(h, w_ref[:, 3072:3584])
    ikw_ref[0] = _dot(h, w_ref[:, 3584:3712])


def _premix(x, mod, nw, w_in_b, tm=512):
    bsz, L, d = x.shape
    tok = lambda w: pl.BlockSpec((1, tm, w), lambda b, i: (b, i, 0))
    return pl.pallas_call(
        _premix_kernel,
        grid=(bsz, L // tm),
        in_specs=[
            tok(d),
            pl.BlockSpec((1, 1, d), lambda b, i: (b, 0, 0)),
            pl.BlockSpec((1, 1, d), lambda b, i: (b, 0, 1)),
            pl.BlockSpec((1, d), lambda b, i: (0, 0)),
            pl.BlockSpec((d, D_IN_PAD), lambda b, i: (0, 0), pipeline_mode=pl.Buffered(1)),
        ],
        out_specs=[tok(2048), tok(256), tok(256), tok(256), tok(256), tok(512), tok(IKW_PAD)],
        out_shape=[
            jax.ShapeDtypeStruct((bsz, L, 2048), f32),
            jax.ShapeDtypeStruct((bsz, L, 256), bf16),
            jax.ShapeDtypeStruct((bsz, L, 256), bf16),
            jax.ShapeDtypeStruct((bsz, L, 256), bf16),
            jax.ShapeDtypeStruct((bsz, L, 256), bf16),
            jax.ShapeDtypeStruct((bsz, L, 512), f32),
            jax.ShapeDtypeStruct((bsz, L, IKW_PAD), f32),
        ],
        compiler_params=pltpu.CompilerParams(
            dimension_semantics=("parallel", "parallel"), vmem_limit_bytes=VMEM_LIMIT),
        name="premix",
    )(x, mod, mod, nw.reshape(1, d), w_in_b)


def _split3(x):
    hi = x.astype(bf16)
    r = x - hi.astype(f32)
    mid = r.astype(bf16)
    lo = (r - mid.astype(f32)).astype(bf16)
    return hi, mid, lo


def _hgrn2_chunk(qr, fr, v, gr, lb, onw, st_ref):
    C, S, W = HG_CHUNK, HG_SUB, HG_DK
    nsub = C // S
    heads = range(HG_HEADS)
    hs = lambda x, h: x[:, h * W:(h + 1) * W]

    q = _silu(qr) * (HG_DK ** -0.5)
    f = lb + (1.0 - lb) * jax.nn.sigmoid(fr)
    lf2 = jnp.log2(jnp.maximum(f, TINY))
    k = (1.0 - lb) * jax.nn.sigmoid(-fr)
    lk2 = jnp.log2(k)

    row = lax.broadcasted_iota(jnp.int32, (C, C), 0)
    col = lax.broadcasted_iota(jnp.int32, (C, C), 1)
    tril = jnp.where(col <= row, 1.0, 0.0).astype(bf16)
    g3 = _dot(tril, jnp.concatenate(_split3(lf2), axis=1))
    nw = HG_HEADS * W
    G2 = g3[:, :nw] + g3[:, nw:2 * nw] + g3[:, 2 * nw:]
    H = lk2 - G2
    g2_last = G2[C - 1:C, :]

    lane = lax.broadcasted_iota(jnp.int32, (S, C), 1)
    srow = lax.broadcasted_iota(jnp.int32, (S, C), 0)
    ones = jnp.ones((W, C), bf16)
    zpad = lambda x: jnp.concatenate([x, jnp.zeros((C - x.shape[0], W), x.dtype)], axis=0)

    o_inter, rsums, offs = [], [], []
    for h in heads:
        qh, gh, hh, lkh = hs(q, h), hs(G2, h), hs(H, h), hs(lk2, h)
        o_inter.append(_dot_nt((qh * jnp.exp2(gh)).astype(bf16), st_ref[h].astype(bf16)))
        parts = []
        for blk in range(nsub):
            r0 = blk * S
            qb, gb = qh[r0:r0 + S], gh[r0:r0 + S]
            for j in range(r0, r0 + S):
                parts.append(qb * jnp.exp2(jnp.minimum(gb + hh[j:j + 1, :], lkh[j:j + 1, :])))
        rsums.append(_dot(jnp.concatenate(parts, axis=0).astype(bf16), ones))
        off_h = []
        for blk in range(1, nsub):
            r0 = blk * S
            gref = gh[r0 - 1:r0, :]
            qd = qh[r0:r0 + S] * jnp.exp2(gh[r0:r0 + S] - gref)
            kd = jnp.exp2(jnp.minimum(gref + hh[:r0], lkh[:r0]))
            off_h.append(_dot_nt(qd.astype(bf16), zpad(kd).astype(bf16)))
        offs.append(off_h)

    outs = []
    for h in heads:
        vh = hs(v, h)
        a_rows = []
        for blk in range(nsub):
            r0 = blk * S
            a = jnp.zeros((S, C), f32)
            for j in range(S):
                a = jnp.where(lane == r0 + j, rsums[h][(r0 + j) * S:(r0 + j + 1) * S, :], a)
            a = jnp.where(lane - r0 <= srow, a, 0.0)
            if blk > 0:
                a = a + offs[h][blk - 1]
            a_rows.append(a)
        A = jnp.concatenate(a_rows, axis=0)
        o = o_inter[h] + _dot(A.astype(bf16), vh.astype(bf16))
        kg = jnp.exp2(hs(g2_last, h) + hs(H, h))
        st_ref[h] = st_ref[h] * jnp.exp2(hs(g2_last, h)) + _dot(vh.T.astype(bf16), kg.astype(bf16))
        o = o * lax.rsqrt(jnp.mean(o * o, axis=-1, keepdims=True) + EPS)
        outs.append(o * hs(onw, h) * _silu(hs(gr, h)))
    return jnp.concatenate(outs, axis=1)


def _hgrn2_kernel(q_ref, f_ref, v_ref, g_ref, lb_ref, onw_ref, o_ref, st_ref, *, n_chunks):
    @pl.when(pl.program_id(1) == 0)
    def _():
        st_ref[...] = jnp.zeros_like(st_ref)

    lb = lb_ref[...]
    onw = onw_ref[...]

    def body(c, carry):
        r0 = pl.multiple_of(c * HG_CHUNK, HG_CHUNK)
        sl = pl.ds(r0, HG_CHUNK)
        o = _hgrn2_chunk(q_ref[0, sl, :], f_ref[0, sl, :], v_ref[0, sl, :], g_ref[0, sl, :], lb, onw, st_ref)
        o_ref[0, sl, :] = o.astype(o_ref.dtype)
        return carry

    lax.fori_loop(0, n_chunks, body, 0)


def _hgrn2(hg, lb, onw, tl=512):
    bsz, L, _ = hg.shape
    part = lambda p: pl.BlockSpec((1, tl, HG_WIDTH), lambda b, i: (b, i, p))
    vec = pl.BlockSpec((1, HG_WIDTH), lambda b, i: (0, 0))
    return pl.pallas_call(
        functools.partial(_hgrn2_kernel, n_chunks=tl // HG_CHUNK),
        grid=(bsz, L // tl),
        in_specs=[part(0), part(1), part(2), part(3), vec, vec],
        out_specs=pl.BlockSpec((1, tl, HG_WIDTH), lambda b, i: (b, i, 0)),
        out_shape=jax.ShapeDtypeStruct((bsz, L, HG_WIDTH), bf16),
        scratch_shapes=[pltpu.VMEM((HG_HEADS, HG_DV, HG_DK), f32)],
        compiler_params=pltpu.CompilerParams(
            dimension_semantics=("parallel", "arbitrary"), vmem_limit_bytes=VMEM_LIMIT),
        name="hgrn2",
    )(hg, hg, hg, hg, lb.reshape(1, -1), onw.reshape(1, -1))


def _sortable(bits):
    return bits ^ ((bits >> 31) & jnp.int32(0x7FFFFFFF))


def _dsa_kernel(qT_ref, k_ref, vT_ref, iqT_ref, ik_ref, wT_ref, o_ref, key_ref, qh_ref,
                mx_ref, den_ref, oT_ref, lga_ref, lgb_ref, p_ref, *, topk, qb, tk, n_qblocks):
    i = pl.program_id(1)
    t0 = i * qb
    nk = (t0 + qb) // tk
    sub = 64
    qpos = t0 + lax.broadcasted_iota(jnp.int32, (1, qb), 1)

    def score_tile(kt, carry, diagonal=False):
        for r in range(tk // sub):
            s0 = pl.multiple_of(kt * tk + r * sub, sub)
            ikt = ik_ref[0, pl.ds(s0, sub), :]
            acc = jnp.zeros((sub, qb), f32)
            for h in range(IDX_HEADS):
                sh = _dot(ikt, iqT_ref[0, h * IDX_DIM:(h + 1) * IDX_DIM, :])
                acc = acc + wT_ref[0, h:h + 1, :] * jnp.maximum(sh, 0.0)
            if diagonal:
                kpos = s0 + lax.broadcasted_iota(jnp.int32, (sub, qb), 0)
                acc = jnp.where(kpos <= qpos, acc, NEG_BIG)
            key_ref[pl.ds(s0, sub), :] = _sortable(pltpu.bitcast(acc + 0.0, jnp.int32))
        return carry

    lax.fori_loop(0, nk - 1, score_tile, 0)
    score_tile(nk - 1, 0, diagonal=True)

    def search(n_tiles):
        def count_ge(cand):
            acc = jnp.zeros((CNT_ROWS // 8, 8, qb), f32)
            for r in range(0, n_tiles * tk, CNT_ROWS):
                keys = key_ref[r:r + CNT_ROWS, :].reshape(CNT_ROWS // 8, 8, qb)
                acc = acc + jnp.where(keys >= cand[None], 1.0, 0.0)
            s = jnp.sum(acc, axis=0)
            for shift in (4, 2, 1):
                s = s + pltpu.roll(s, shift, 0)
            return s

        def step(it, carry):
            v, c_ge, c_gt = carry
            trial = v + (jnp.int32(1) << (31 - it))
            cnt = count_ge(trial)
            ok = cnt >= topk
            return jnp.where(ok, trial, v), jnp.where(ok, cnt, c_ge), jnp.where(ok, c_gt, cnt)

        v, c_ge, c_gt = lax.fori_loop(
            0, 32, step,
            (jnp.full((8, qb), -2 ** 31, jnp.int32), jnp.full((8, qb), float(n_tiles * tk), f32),
             jnp.zeros((8, qb), f32)))
        return v[0:1], c_ge[0:1], c_gt[0:1]

    zeros_i = jnp.zeros((1, qb), jnp.int32)
    ustar, c_ge, c_gt = lax.switch(i, [functools.partial(search, n + 1) for n in range(n_qblocks)])
    need = topk - c_gt
    cnt_eq = c_ge - c_gt
    take_all = qpos < topk

    def tie_limit(_):
        def step(it, p):
            trial = p | (jnp.int32(1) << (12 - it))

            def tile(kt, acc):
                s0 = pl.multiple_of(kt * tk, tk)
                kpos = s0 + lax.broadcasted_iota(jnp.int32, (tk, qb), 0)
                tied = jnp.where(kpos < trial, jnp.where(key_ref[pl.ds(s0, tk), :] == ustar, 1.0, 0.0), 0.0)
                return acc + jnp.sum(tied.reshape(tk // CNT_ROWS, CNT_ROWS, qb), axis=0)
            acc = lax.fori_loop(0, nk, tile, jnp.zeros((CNT_ROWS, qb), f32))
            return jnp.where(jnp.sum(acc, axis=0, keepdims=True) < need, trial, p)
        return lax.fori_loop(0, 13, step, zeros_i)

    has_excess = jnp.max(jnp.where(take_all, 0.0, cnt_eq - need)) > 0.0
    plim = lax.cond(has_excess, tie_limit, lambda _: jnp.full((1, qb), 2 ** 30, jnp.int32), 0)

    def bias_tile(kt, carry):
        s0 = pl.multiple_of(kt * tk, tk)
        u = key_ref[pl.ds(s0, tk), :]
        kpos = s0 + lax.broadcasted_iota(jnp.int32, (tk, qb), 0)
        tie = jnp.where(kpos <= plim, 0.0, NEG_BIG)
        bias = jnp.where(u > ustar, 0.0, jnp.where(u == ustar, tie, NEG_BIG))
        bias = jnp.where(take_all, 0.0, bias)
        key_ref[pl.ds(s0, tk), :] = pltpu.bitcast(jnp.where(kpos <= qpos, bias, NEG_BIG), jnp.int32)
        return carry

    lax.fori_loop(0, nk, bias_tile, 0)

    q_all = qT_ref[0] * jnp.asarray(SA_HEAD_DIM ** -0.5, bf16)
    hrow = lax.broadcasted_iota(jnp.int32, q_all.shape, 0) // SA_HEAD_DIM
    for h in range(SA_HEADS):
        qh_ref[:, h * qb:(h + 1) * qb] = jnp.where(hrow == h, q_all, jnp.zeros_like(q_all))
    mx_ref[...] = jnp.full(mx_ref.shape, NEG_BIG, f32)
    den_ref[...] = jnp.zeros_like(den_ref)
    oT_ref[...] = jnp.zeros_like(oT_ref)

    ta = tk // 2
    n_att = 2 * nk

    def logits(t, dst):
        s0 = pl.multiple_of(t * ta, ta)
        dst[...] = _dot(k_ref[0, pl.ds(s0, ta), :], qh_ref[...])

    def softmax_pv(t, src):
        s0 = pl.multiple_of(t * ta, ta)
        alphas = []
        for h in range(SA_HEADS):
            cols = slice(h * qb, (h + 1) * qb)
            mt = jnp.full((8, qb), NEG_BIG, f32)
            for r in range(0, ta, ATT_STRIP):
                lg = src[r:r + ATT_STRIP, cols] + pltpu.bitcast(key_ref[pl.ds(s0 + r, ATT_STRIP), :], f32)
                src[r:r + ATT_STRIP, cols] = lg
                mt = jnp.maximum(mt, jnp.max(lg.reshape(ATT_STRIP // 8, 8, qb), axis=0))
            m_old = mx_ref[h:h + 1, :]
            m_new = jnp.maximum(m_old, jnp.max(mt, axis=0, keepdims=True))
            alphas.append(jnp.exp(m_old - m_new))
            mx_ref[h:h + 1, :] = m_new
            dsum = jnp.zeros((8, qb), f32)
            for r in range(0, ta, ATT_STRIP):
                p = jnp.exp(src[r:r + ATT_STRIP, cols] - m_new)
                dsum = dsum + jnp.sum(p.reshape(ATT_STRIP // 8, 8, qb), axis=0)
                p_ref[h, r:r + ATT_STRIP, :] = p.astype(bf16)
            den_ref[h:h + 1, :] = alphas[h] * den_ref[h:h + 1, :] + jnp.sum(dsum, axis=0, keepdims=True)
        for h in range(SA_HEADS):
            hs = slice(h * SA_HEAD_DIM, (h + 1) * SA_HEAD_DIM)
            oT_ref[hs, :] = alphas[h] * oT_ref[hs, :] + _dot(vT_ref[0, hs, pl.ds(s0, ta)], p_ref[h])

    logits(0, lga_ref)

    def att_pair(j, carry):
        logits(2 * j + 1, lgb_ref)
        softmax_pv(2 * j, lga_ref)
        logits(jnp.minimum(2 * j + 2, n_att - 1), lga_ref)
        softmax_pv(2 * j + 1, lgb_ref)
        return carry

    lax.fori_loop(0, nk, att_pair, 0)
    for h in range(SA_HEADS):
        hs = slice(h * SA_HEAD_DIM, (h + 1) * SA_HEAD_DIM)
        oT_ref[hs, :] = oT_ref[hs, :] / den_ref[h:h + 1, :]
    o_ref[0] = oT_ref[...].T.astype(o_ref.dtype)


def _dsa(qT, k, vT, iqT, ik, wT, qb=512):
    bsz, L, _ = k.shape
    topk = min(IDX_TOPK_MAX, L // IDX_TOPK_DIV)
    assert L % qb == 0 and qb >= topk
    whole = lambda shape: pl.BlockSpec(shape, lambda b, i: (b, 0, 0), pipeline_mode=pl.Buffered(1))
    return pl.pallas_call(
        functools.partial(_dsa_kernel, topk=topk, qb=qb, tk=qb, n_qblocks=L // qb),
        grid=(bsz, L // qb),
        in_specs=[
            pl.BlockSpec((1, SA_WIDTH, qb), lambda b, i: (b, 0, i)),
            whole((1, L, SA_WIDTH)),
            whole((1, SA_WIDTH, L)),
            pl.BlockSpec((1, IDX_HEADS * IDX_DIM, qb), lambda b, i: (b, 0, i)),
            whole((1, L, IDX_DIM)),
            pl.BlockSpec((1, IDX_HEADS, qb), lambda b, i: (b, 0, i)),
        ],
        out_specs=pl.BlockSpec((1, qb, SA_WIDTH), lambda b, i: (b, i, 0)),
        out_shape=jax.ShapeDtypeStruct((bsz, L, SA_WIDTH), bf16),
        scratch_shapes=[pltpu.VMEM((L, qb), jnp.int32),
                        pltpu.VMEM((SA_WIDTH, SA_HEADS * qb), bf16),
                        pltpu.VMEM((8, qb), f32), pltpu.VMEM((8, qb), f32),
                        pltpu.VMEM((SA_WIDTH, qb), f32),
                        pltpu.VMEM((qb // 2, SA_HEADS * qb), f32), pltpu.VMEM((qb // 2, SA_HEADS * qb), f32),
                        pltpu.VMEM((SA_HEADS, qb // 2, qb), bf16)],
        compiler_params=pltpu.CompilerParams(
            dimension_semantics=("parallel", "arbitrary"), vmem_limit_bytes=VMEM_LIMIT),
        name="dsa",
    )(qT, k, vT, iqT, ik, wT)


def _conv_kernel(cu_ref, halo_ref, w_ref, b_ref, lnw_ref, lnb_ref, o_ref, h_ref, *, tl, strip):
    glu = lambda u: u[:, :CV_WIDTH] * jax.nn.sigmoid(u[:, CV_WIDTH:])
    halo = glu(halo_ref[0])
    h_ref[0, 0:CV_HALO, :] = jnp.where(pl.program_id(1) > 0, halo, 0.0)
    h_ref[0, CV_HALO:, :] = glu(cu_ref[0])
    for s in range(1, 8):
        h_ref[s, 0:tl + CV_HALO - 8, :] = h_ref[0, s:s + tl + CV_HALO - 8, :]
    lead = CV_HALO - (CV_FILTER - 1)
    for r in range(tl // strip):
        acc = jnp.zeros((strip, CV_WIDTH), f32) + b_ref[...]
        for j in range(CV_FILTER):
            s, base = (lead + j) % 8, (lead + j) // 8 * 8
            acc = acc + w_ref[j:j + 1, :] * h_ref[s, r * strip + base:r * strip + base + strip, :]
        mu = jnp.mean(acc, axis=-1, keepdims=True)
        cen = acc - mu
        var = jnp.mean(cen * cen, axis=-1, keepdims=True)
        y = cen * lax.rsqrt(var + EPS) * lnw_ref[...] + lnb_ref[...]
        o_ref[0, r * strip:(r + 1) * strip, :] = _silu(y).astype(o_ref.dtype)


def _conv(cu, cv_w, cv_b, ln_w, ln_b, tl=512, strip=64):
    bsz, L, _ = cu.shape
    hb = tl // CV_HALO
    vec = pl.BlockSpec((1, CV_WIDTH), lambda b, i: (0, 0))
    return pl.pallas_call(
        functools.partial(_conv_kernel, tl=tl, strip=strip),
        grid=(bsz, L // tl),
        in_specs=[
            pl.BlockSpec((1, tl, 2 * CV_WIDTH), lambda b, i: (b, i, 0)),
            pl.BlockSpec((1, CV_HALO, 2 * CV_WIDTH), lambda b, i: (b, jnp.maximum(i * hb - 1, 0), 0)),
            pl.BlockSpec((CV_FILTER, CV_WIDTH), lambda b, i: (0, 0)),
            vec, vec, vec,
        ],
        out_specs=pl.BlockSpec((1, tl, CV_WIDTH), lambda b, i: (b, i, 0)),
        out_shape=jax.ShapeDtypeStruct((bsz, L, CV_WIDTH), bf16),
        scratch_shapes=[pltpu.VMEM((8, tl + CV_HALO, CV_WIDTH), f32)],
        compiler_params=pltpu.CompilerParams(dimension_semantics=("parallel", "parallel")),
        name="conv",
    )(cu, cu, cv_w, cv_b.reshape(1, -1), ln_w.reshape(1, -1), ln_b.reshape(1, -1))


def _postmix_kernel(x_ref, oa_ref, ob_ref, oc_ref, g1_ref, sh_ref, sc_ref, g2_ref, nw_ref,
                    wo_ref, w1_ref, w2_ref, fw_ref, o_ref, acc_ref, *, ff_chunk, final):
    mix = (_dot(oa_ref[0], wo_ref[0:HG_WIDTH, :])
           + _dot(ob_ref[0], wo_ref[HG_WIDTH:HG_WIDTH + SA_WIDTH, :])
           + _dot(oc_ref[0], wo_ref[HG_WIDTH + SA_WIDTH:, :]))
    x1 = x_ref[0] + g1_ref[0] * mix
    h = _modulated_norm(x1, nw_ref[...], sc_ref[0], sh_ref[0]).astype(bf16)
    for c in range(D_FF // ff_chunk):
        u = jnp.maximum(_dot(h, w1_ref[:, c * ff_chunk:(c + 1) * ff_chunk]), 0.0)
        y = _dot((u * u).astype(bf16), w2_ref[c * ff_chunk:(c + 1) * ff_chunk, :])
        if c == 0:
            acc_ref[...] = y
        else:
            acc_ref[...] += y
    x2 = x1 + g2_ref[0] * acc_ref[...]
    if final:
        ms = jnp.mean(x2 * x2, axis=-1, keepdims=True)
        x2 = x2 * lax.rsqrt(ms + EPS) * fw_ref[...]
    o_ref[0] = x2


def _postmix(x, o_a, o_b, o_c, mod, nw, wo_b, w1_b, w2_b, fw, final, tm=512, ff_chunk=1024):
    bsz, L, d = x.shape
    tok = lambda w: pl.BlockSpec((1, tm, w), lambda b, i: (b, i, 0))
    modp = lambda j: pl.BlockSpec((1, 1, d), lambda b, i: (b, 0, j))
    const = lambda shape: pl.BlockSpec(shape, lambda b, i: (0, 0), pipeline_mode=pl.Buffered(1))
    return pl.pallas_call(
        functools.partial(_postmix_kernel, ff_chunk=ff_chunk, final=final),
        grid=(bsz, L // tm),
        in_specs=[
            tok(d), tok(HG_WIDTH), tok(SA_WIDTH), tok(CV_WIDTH),
            modp(2), modp(3), modp(4), modp(5),
            pl.BlockSpec((1, d), lambda b, i: (0, 0)),
            const((d, d)), const((d, D_FF)), const((D_FF, d)),
            pl.BlockSpec((1, d), lambda b, i: (0, 0)),
        ],
        out_specs=tok(d),
        out_shape=jax.ShapeDtypeStruct((bsz, L, d), f32),
        scratch_shapes=[pltpu.VMEM((tm, d), f32)],
        compiler_params=pltpu.CompilerParams(
            dimension_semantics=("parallel", "parallel"), vmem_limit_bytes=VMEM_LIMIT),
        name="postmix",
    )(x, o_a, o_b, o_c, mod, mod, mod, mod, nw.reshape(1, d), wo_b, w1_b, w2_b, fw.reshape(1, d))


def _pack_w_in(w_in):
    pad = jnp.zeros(w_in.shape[:-1] + (D_IN_PAD - w_in.shape[-1],), w_in.dtype)
    return jnp.concatenate([w_in[..., :3072], w_in[..., 3112:], w_in[..., 3072:3112], pad], axis=-1).astype(bf16)


def _lower_bounds(lb_logits):
    p = jax.nn.softmax(lb_logits.astype(f32), axis=0)
    return jnp.cumsum(p, axis=0) - p[0]


def kernel(x, c, ada_w, ada_b, norm_mix_w, norm_mlp_w, w_in, hg_lb_logits, hg_onorm_w, cv_w, cv_b,
           cv_ln_w, cv_ln_b, w_out, mlp_w1, mlp_w2, final_norm_w):
    bsz, L, d = x.shape
    depth = w_in.shape[0]
    mods = _ada(c, ada_w, ada_b)
    lbs = _lower_bounds(hg_lb_logits)
    w_in_b = _pack_w_in(w_in)
    wo_b, w1_b, w2_b = w_out.astype(bf16), mlp_w1.astype(bf16), mlp_w2.astype(bf16)
    tr = lambda a: jnp.swapaxes(a, 1, 2)
    for l in range(depth):
        mod = mods[l].reshape(bsz, 1, N_MOD * d)
        hg, sq, sk, sv, iq, cu, ikw = _premix(x, mod, norm_mix_w[l], w_in_b[l])
        o_a = _hgrn2(hg, lbs[l], hg_onorm_w[l])
        o_b = _dsa(tr(sq), sk, tr(sv), tr(iq), ikw[..., :IDX_DIM].astype(bf16),
                   tr(ikw[..., IDX_DIM:IDX_DIM + IDX_HEADS]))
        o_c = _conv(cu, cv_w[l], cv_b[l], cv_ln_w[l], cv_ln_b[l])
        x = _postmix(x, o_a, o_b, o_c, mod, norm_mlp_w[l], wo_b[l], w1_b[l], w2_b[l],
                     final_norm_w, final=(l == depth - 1))
    return x
```
